```python
import jax, jax.numpy as jnp
from jax import lax
import numpy as np

D_MODEL = 1024
BATCH = 8
SEQ = 2048
DEPTH = 1
DEC_BATCH = 128
DEC_SEQ = 8
PAST_LEN = 16384
PAGE_SIZE = 128

D_CONV = D_MODEL
CONV_WIDTH = 3
POOL_WINDOWS = (2, 4, 8, 16)
N_POOL_GROUPS = len(POOL_WINDOWS)
D_POOL = D_MODEL
POOL_GROUP = D_POOL // N_POOL_GROUPS
POOL_BUF = max(POOL_WINDOWS) - 1
D_FF = 4 * D_MODEL
D_IN_ALL = 3 * D_CONV + D_POOL + 2 * D_MODEL
EPS = 1e-6

kernel_name = "gated_conv_pool_hybrid_step"


def rmsnorm(x, g):
    xf = x.astype(jnp.float32)
    r = lax.rsqrt(jnp.mean(xf * xf, axis=-1, keepdims=True) + EPS)
    return (xf * r * g.astype(jnp.float32)).astype(x.dtype)


def short_gated_conv(xv, b, c, buf, w_conv):
    T = xv.shape[1]
    z = c * xv
    zc = jnp.concatenate([buf.astype(z.dtype), z], axis=1)
    y = zc[:, 0:T] * w_conv[0]
    for k in range(1, CONV_WIDTH):
        y = y + zc[:, k:k + T] * w_conv[k]
    return b * y, zc[:, -(CONV_WIDTH - 1):]


def multiscale_pool(u, buf, start_pos, w_pool_group, pool_scale):
    Bsz, T, _ = u.shape
    uc = jnp.concatenate([buf.astype(u.dtype), u], axis=1)
    cs = jnp.cumsum(uc.astype(jnp.float32), axis=1)
    cs = jnp.concatenate([jnp.zeros((Bsz, 1, D_POOL), jnp.float32), cs], axis=1)
    pos = start_pos + jnp.arange(T, dtype=jnp.int32)
    uf = u.astype(jnp.float32)
    off = POOL_BUF + 1
    groups = []
    for gi, w in enumerate(POOL_WINDOWS):
        sl = slice(gi * POOL_GROUP, (gi + 1) * POOL_GROUP)
        win_sum = cs[:, off:off + T, sl] - cs[:, off - w:off - w + T, sl]
        count = jnp.minimum(w, pos + 1).astype(jnp.float32)[None, :, None]
        groups.append(win_sum / count - uf[:, :, sl])
    pooled = jnp.stack(groups, axis=2)
    y = jnp.einsum('btgi,gio->btgo', pooled, w_pool_group.astype(jnp.float32))
    y = y.reshape(Bsz, T, D_POOL) * pool_scale.astype(jnp.float32)
    return y.astype(u.dtype), uc[:, -POOL_BUF:]


def layer(x, conv_buf, pool_buf, start_pos, g_pre_mix, w_in, b_gate, w_conv, w_out_conv,
          w_pool_group, pool_scale, w_o, g_post_mix, g_pre_mlp, w_up, w_down, g_post_mlp):
    h = rmsnorm(x, g_pre_mix)
    p = h @ w_in
    o = 0
    xv = p[..., o:o + D_CONV]; o += D_CONV
    bg = p[..., o:o + D_CONV]; o += D_CONV
    cg = p[..., o:o + D_CONV]; o += D_CONV
    u = p[..., o:o + D_POOL]; o += D_POOL
    gate_pre = p[..., o:o + 2 * D_MODEL]

    ya, conv_new = short_gated_conv(xv, bg, cg, conv_buf, w_conv)
    ya = ya @ w_out_conv
    yb, pool_new = multiscale_pool(u, pool_buf, start_pos, w_pool_group, pool_scale)

    gates = jax.nn.sigmoid((gate_pre + b_gate).astype(jnp.float32)).astype(x.dtype)
    merged = gates[..., :D_MODEL] * ya + gates[..., D_MODEL:] * yb
    x = x + rmsnorm(merged @ w_o, g_post_mix)

    h2 = rmsnorm(x, g_pre_mlp)
    f = jnp.square(jax.nn.relu(h2 @ w_up)) @ w_down
    x = x + rmsnorm(f, g_post_mlp)
    return x, conv_new, pool_new


def setup_inputs(seed: int = 0) -> dict:
    key = jax.random.key(seed)
    ks = jax.random.split(key, 16)
    n = jax.random.normal
    f32 = jnp.float32
    return {
        "x_prompt": n(ks[0], (BATCH, SEQ, D_MODEL), f32),
        "x_sample": n(ks[1], (DEC_BATCH, DEC_SEQ, D_MODEL), f32),
        "state_conv": n(ks[2], (DEPTH, DEC_BATCH, CONV_WIDTH - 1, D_CONV), f32),
        "state_pool": n(ks[3], (DEPTH, DEC_BATCH, POOL_BUF, D_POOL), f32),
        "g_pre_mix": 1.0 + 0.05 * n(ks[4], (DEPTH, D_MODEL), f32),
        "w_in": n(ks[5], (DEPTH, D_MODEL, D_IN_ALL), f32) * D_MODEL ** -0.5,
        "b_gate": 0.1 * n(ks[6], (DEPTH, 2 * D_MODEL), f32),
        "w_conv": n(ks[7], (DEPTH, CONV_WIDTH, D_CONV), f32) * CONV_WIDTH ** -0.5,
        "w_out_conv": n(ks[8], (DEPTH, D_CONV, D_MODEL), f32) * D_CONV ** -0.5,
        "w_pool_group": n(ks[9], (DEPTH, N_POOL_GROUPS, POOL_GROUP, POOL_GROUP), f32) * POOL_GROUP ** -0.5,
        "pool_scale": 1.0 + 0.05 * n(ks[10], (DEPTH, D_POOL), f32),
        "w_o": n(ks[11], (DEPTH, D_MODEL, D_MODEL), f32) * D_MODEL ** -0.5,
        "g_post_mix": 1.0 + 0.05 * n(ks[12], (DEPTH, D_MODEL), f32),
        "g_pre_mlp": 1.0 + 0.05 * n(ks[13], (DEPTH, D_MODEL), f32),
        "w_up": n(ks[14], (DEPTH, D_MODEL, D_FF), f32) * D_MODEL ** -0.5,
        "w_down": n(ks[15], (DEPTH, D_FF, D_MODEL), f32) * D_FF ** -0.5,
        "g_post_mlp": 1.0 + 0.05 * n(jax.random.fold_in(key, 99), (DEPTH, D_MODEL), f32),
    }


def reference(x_prompt, x_sample, state_conv, state_pool, g_pre_mix, w_in, b_gate, w_conv,
              w_out_conv, w_pool_group, pool_scale, w_o, g_post_mix, g_pre_mlp, w_up, w_down,
              g_post_mlp):
    yp, ys = x_prompt, x_sample
    nb_p = x_prompt.shape[0]
    conv_p, pool_p, conv_s, pool_s = [], [], [], []
    for l in range(DEPTH):
        params = (g_pre_mix[l], w_in[l], b_gate[l], w_conv[l], w_out_conv[l], w_pool_group[l],
                  pool_scale[l], w_o[l], g_post_mix[l], g_pre_mlp[l], w_up[l], w_down[l],
                  g_post_mlp[l])
        zero_conv = jnp.zeros((nb_p, CONV_WIDTH - 1, D_CONV), x_prompt.dtype)
        zero_pool = jnp.zeros((nb_p, POOL_BUF, D_POOL), x_prompt.dtype)
        yp, cp, pp = layer(yp, zero_conv, zero_pool, 0, *params)
        ys, cs_, ps = layer(ys, state_conv[l], state_pool[l], PAST_LEN, *params)
        conv_p.append(cp); pool_p.append(pp); conv_s.append(cs_); pool_s.append(ps)
    new_conv_prompt = jnp.stack(conv_p, axis=0)
    new_pool_prompt = jnp.stack(pool_p, axis=0)
    new_conv_sample = jnp.stack(conv_s, axis=0)
    new_pool_sample = jnp.stack(pool_s, axis=0)
    return (yp, ys, new_conv_prompt, new_pool_prompt, new_conv_sample, new_pool_sample)
```

```python
import functools

import jax
import jax.numpy as jnp
from jax.experimental import pallas as pl
from jax.experimental.pallas import tpu as pltpu

D_MODEL = 1024
D_FF = 4 * D_MODEL
CONV_WIDTH = 3
POOL_WINDOWS = (2, 4, 8, 16)
POOL_GROUP = D_MODEL // len(POOL_WINDOWS)
POOL_BUF = max(POOL_WINDOWS) - 1
PAST_LEN = 16384
EPS = 1e-6

CONV_HIST = 8
POOL_HIST = 16

PROMPT_TILE = 256
SAMPLE_SEQS = 32
FF_CHUNK = 1024
VMEM_LIMIT_BYTES = 58 * 1024 * 1024


def _rmsnorm(x, g):
    r = jax.lax.rsqrt(jnp.mean(x * x, axis=-1, keepdims=True) + EPS)
    return x * r * g


def _dot(a, w):
    return jnp.dot(a.astype(jnp.bfloat16), w, preferred_element_type=jnp.float32)


def _layer(x, z_hist, u_hist, start_pos, nseq, t_len, w):
    (g_pre_mix, w_in, b_gate, w_conv, w_out_conv, w_pool, pool_scale, w_o,
     g_post_mix, g_pre_mlp, w_up, w_down, g_post_mlp) = w
    rows = nseq * t_len
    d = D_MODEL

    h = _rmsnorm(x, g_pre_mix[...]).astype(jnp.bfloat16)

    xv = _dot(h, w_in[:, 0 * d:1 * d])
    cg = _dot(h, w_in[:, 2 * d:3 * d])
    z = (cg * xv).reshape(nseq, t_len, d)
    zext = jnp.concatenate([z_hist, z], axis=1)
    wc = w_conv[...]
    conv = zext * wc[2:3, :][None]
    conv = conv + pltpu.roll(zext, 1, axis=1) * wc[1:2, :][None]
    conv = conv + pltpu.roll(zext, 2, axis=1) * wc[0:1, :][None]
    conv = conv[:, CONV_HIST:, :].reshape(rows, d)
    bg = _dot(h, w_in[:, 1 * d:2 * d])
    ya = _dot(bg * conv, w_out_conv[...])

    u = _dot(h, w_in[:, 3 * d:4 * d]).reshape(nseq, t_len, d)
    uext = jnp.concatenate([u_hist, u], axis=1)
    pos = start_pos + jax.lax.broadcasted_iota(jnp.int32, (1, t_len, 1), 1)
    yb_groups = []
    for gi, win in enumerate(POOL_WINDOWS):
        sl = slice(gi * POOL_GROUP, (gi + 1) * POOL_GROUP)
        s = uext[:, :, sl]
        k = 1
        while k < win:
            s = s + pltpu.roll(s, k, axis=1)
            k *= 2
        inv_count = 1.0 / jnp.minimum(win, pos + 1).astype(jnp.float32)
        pooled = s[:, POOL_HIST:, :] * inv_count - u[:, :, sl]
        yb_groups.append(_dot(pooled.reshape(rows, POOL_GROUP), w_pool[gi]))
    yb = jnp.concatenate(yb_groups, axis=-1) * pool_scale[...]

    bgate = b_gate[...]
    g_conv = jax.nn.sigmoid(_dot(h, w_in[:, 4 * d:5 * d]) + bgate[:, :d])
    g_pool = jax.nn.sigmoid(_dot(h, w_in[:, 5 * d:6 * d]) + bgate[:, d:])
    merged = g_conv * ya + g_pool * yb
    x1 = x + _rmsnorm(_dot(merged, w_o[...]), g_post_mix[...])

    h2 = _rmsnorm(x1, g_pre_mlp[...]).astype(jnp.bfloat16)
    f = None
    for c in range(D_FF // FF_CHUNK):
        cs = slice(c * FF_CHUNK, (c + 1) * FF_CHUNK)
        a = jnp.square(jnp.maximum(_dot(h2, w_up[:, cs]), 0.0))
        part = _dot(a, w_down[cs, :])
        f = part if f is None else f + part
    y = x1 + _rmsnorm(f, g_post_mlp[...])
    return y, z, u


def _prompt_kernel(x_ref, *refs):
    w = refs[:13]
    y_ref, conv_ref, pool_ref, z_tail, u_tail = refs[13:]
    j = pl.program_id(1)
    t_len = x_ref.shape[0]

    @pl.when(j == 0)
    def _():
        z_tail[...] = jnp.zeros_like(z_tail)
        u_tail[...] = jnp.zeros_like(u_tail)

    y, z, u = _layer(x_ref[...], z_tail[...][None], u_tail[...][None],
                     j * t_len, 1, t_len, w)
    y_ref[...] = y
    z_last = z[0, t_len - CONV_HIST:, :]
    u_last = u[0, t_len - POOL_HIST:, :]
    z_tail[...] = z_last
    u_tail[...] = u_last
    conv_ref[...] = z_last
    pool_ref[...] = u_last


def _sample_kernel(x_ref, zs_ref, us_ref, *refs):
    w = refs[:13]
    y_ref, conv_ref, pool_ref = refs[13:]
    nseq = us_ref.shape[0]
    t_len = x_ref.shape[0] // nseq
    z_hist = zs_ref[...].reshape(nseq, CONV_HIST, D_MODEL)
    y, z, u = _layer(x_ref[...], z_hist, us_ref[...], PAST_LEN, nseq, t_len, w)
    y_ref[...] = y
    conv_ref[...] = z.reshape(nseq * t_len, D_MODEL)
    pool_ref[...] = u.reshape(nseq * t_len, D_MODEL)


def _weight_specs(n_grid_axes):
    def const(shape):
        zeros = (0,) * len(shape)
        if n_grid_axes == 1:
            index_map = lambda i: zeros
        else:
            index_map = lambda b, j: zeros
        return pl.BlockSpec(shape, index_map, pipeline_mode=pl.Buffered(1))
    d = D_MODEL
    return [
        const((1, d)),
        const((d, 6 * d)),
        const((1, 2 * d)),
        const((CONV_WIDTH, d)),
        const((d, d)),
        const((len(POOL_WINDOWS), POOL_GROUP, POOL_GROUP)),
        const((1, d)),
        const((d, d)),
        const((1, d)),
        const((1, d)),
        const((d, D_FF)),
        const((D_FF, d)),
        const((1, d)),
    ]


def _prompt_call(x, weights):
    batch, seq, d = x.shape
    tm = PROMPT_TILE
    f32 = jnp.float32
    return pl.pallas_call(
        _prompt_kernel,
        grid=(batch, seq // tm),
        in_specs=[pl.BlockSpec((None, tm, d), lambda b, j: (b, j, 0))] + _weight_specs(2),
        out_specs=[
            pl.BlockSpec((None, tm, d), lambda b, j: (b, j, 0)),
            pl.BlockSpec((None, CONV_HIST, d), lambda b, j: (b, 0, 0)),
            pl.BlockSpec((None, POOL_HIST, d), lambda b, j: (b, 0, 0)),
        ],
        out_shape=[
            jax.ShapeDtypeStruct((batch, seq, d), f32),
            jax.ShapeDtypeStruct((batch, CONV_HIST, d), f32),
            jax.ShapeDtypeStruct((batch, POOL_HIST, d), f32),
        ],
        scratch_shapes=[pltpu.VMEM((CONV_HIST, d), f32), pltpu.VMEM((POOL_HIST, d), f32)],
        compiler_params=pltpu.CompilerParams(
            dimension_semantics=("arbitrary", "arbitrary"),
            vmem_limit_bytes=VMEM_LIMIT_BYTES),
        name="prompt_layer",
    )(x, *weights)


def _sample_call(x2d, z_state, u_state, weights):
    rows, d = x2d.shape
    nseq_total = u_state.shape[0]
    t_len = rows // nseq_total
    ns = SAMPLE_SEQS
    f32 = jnp.float32
    return pl.pallas_call(
        _sample_kernel,
        grid=(nseq_total // ns,),
        in_specs=[
            pl.BlockSpec((ns * t_len, d), lambda i: (i, 0)),
            pl.BlockSpec((ns * CONV_HIST, d), lambda i: (i, 0)),
            pl.BlockSpec((ns, POOL_HIST, d), lambda i: (i, 0, 0)),
        ] + _weight_specs(1),
        out_specs=[
            pl.BlockSpec((ns * t_len, d), lambda i: (i, 0)),
            pl.BlockSpec((ns * t_len, d), lambda i: (i, 0)),
            pl.BlockSpec((ns * t_len, d), lambda i: (i, 0)),
        ],
        out_shape=[jax.ShapeDtypeStruct((rows, d), f32)] * 3,
        compiler_params=pltpu.CompilerParams(
            dimension_semantics=("arbitrary",),
            vmem_limit_bytes=VMEM_LIMIT_BYTES),
        name="sample_layer",
    )(x2d, z_state, u_state, *weights)


def kernel(x_prompt, x_sample, state_conv, state_pool, g_pre_mix, w_in, b_gate, w_conv,
           w_out_conv, w_pool_group, pool_scale, w_o, g_post_mix, g_pre_mlp, w_up, w_down,
           g_post_mlp):
    depth = w_in.shape[0]
    assert depth == 1, "single-layer step"
    bf16 = jnp.bfloat16
    weights = (
        g_pre_mix, w_in[0].astype(bf16), b_gate, w_conv[0], w_out_conv[0].astype(bf16),
        w_pool_group[0].astype(bf16), pool_scale, w_o[0].astype(bf16), g_post_mix,
        g_pre_mlp, w_up[0].astype(bf16), w_down[0].astype(bf16), g_post_mlp)

    y_prompt, conv_tail, pool_tail = _prompt_call(x_prompt, weights)
    new_conv_prompt = conv_tail[None, :, CONV_HIST - (CONV_WIDTH - 1):, :]
    new_pool_prompt = pool_tail[None, :, POOL_HIST - POOL_BUF:, :]

    nseq, t_len, d = x_sample.shape
    z_state = jnp.pad(state_conv[0], ((0, 0), (CONV_HIST - (CONV_WIDTH - 1), 0), (0, 0)))
    u_state = jnp.pad(state_pool[0], ((0, 0), (POOL_HIST - POOL_BUF, 0), (0, 0)))
    y_s, z_s, u_s = _sample_call(
        x_sample.reshape(nseq * t_len, d), z_state.reshape(nseq * CONV_HIST, d), u_state, weights)
    y_sample = y_s.reshape(nseq, t_len, d)
    zc = jnp.concatenate([state_conv[0], z_s.reshape(nseq, t_len, d)], axis=1)
    uc = jnp.concatenate([state_pool[0], u_s.reshape(nseq, t_len, d)], axis=1)
    new_conv_sample = zc[None, :, -(CONV_WIDTH - 1):, :]
    new_pool_sample = uc[None, :, -POOL_BUF:, :]
    return (y_prompt, y_sample, new_conv_prompt, new_pool_prompt, new_conv_sample,
            new_pool_sample)
```

```python
import jax
import jax.numpy as jnp
from jax.experimental import pallas as pl
from jax.experimental.pallas import tpu as pltpu

D_MODEL = 1024
D_FF = 4 * D_MODEL
CONV_WIDTH = 3
POOL_WINDOWS = (2, 4, 8, 16)
POOL_GROUP = D_MODEL // len(POOL_WINDOWS)
POOL_BUF = max(POOL_WINDOWS) - 1
PAST_LEN = 16384
EPS = 1e-6

CONV_HIST = 8
POOL_HIST = 16

PROMPT_TILE = 256
PROMPT_SEQS = 2
SAMPLE_SEQS = 32
FF_CHUNK = 1024
VMEM_LIMIT_BYTES = 58 * 1024 * 1024


def _rmsnorm(x, g):
    r = jax.lax.rsqrt(jnp.mean(x * x, axis=-1, keepdims=True) + EPS)
    return x * r * g


def _dot(a, w):
    return jnp.dot(a.astype(jnp.bfloat16), w, preferred_element_type=jnp.float32)


def _layer_stages(w):
    (g_pre_mix, w_in, b_gate, w_conv, w_out_conv, w_pool, pool_scale, w_o,
     g_post_mix, g_pre_mlp, w_up, w_down, g_post_mlp) = w
    d = D_MODEL

    def pre_norm(st):
        st["h"] = _rmsnorm(st["x"], g_pre_mix[...]).astype(jnp.bfloat16)

    def conv_branch(st):
        nseq, t_len, h = st["nseq"], st["t_len"], st["h"]
        xv = _dot(h, w_in[:, 0 * d:1 * d])
        cg = _dot(h, w_in[:, 2 * d:3 * d])
        bg = _dot(h, w_in[:, 1 * d:2 * d])
        z = (cg * xv).reshape(nseq, t_len, d)
        zext = jnp.concatenate([st["z_hist"], z], axis=1)
        wc = w_conv[...]
        conv = zext * wc[2:3, :][None]
        conv = conv + pltpu.roll(zext, 1, axis=1) * wc[1:2, :][None]
        conv = conv + pltpu.roll(zext, 2, axis=1) * wc[0:1, :][None]
        conv = conv[:, CONV_HIST:, :].reshape(nseq * t_len, d)
        st["z"] = z
        st["yc"] = (bg * conv).astype(jnp.bfloat16)

    def pool_branch(st):
        nseq, t_len, h = st["nseq"], st["t_len"], st["h"]
        st["ya"] = _dot(st.pop("yc"), w_out_conv[...])
        u = _dot(h, w_in[:, 3 * d:4 * d]).reshape(nseq, t_len, d)
        uext = jnp.concatenate([st["u_hist"], u], axis=1)
        pos = st["start_pos"] + jax.lax.broadcasted_iota(jnp.int32, (1, t_len, 1), 1)
        pooled = []
        for gi, win in enumerate(POOL_WINDOWS):
            sl = slice(gi * POOL_GROUP, (gi + 1) * POOL_GROUP)
            s = uext[:, :, sl]
            k = 1
            while k < win:
                s = s + pltpu.roll(s, k, axis=1)
                k *= 2
            inv_count = 1.0 / jnp.minimum(win, pos + 1).astype(jnp.float32)
            p = s[:, POOL_HIST:, :] * inv_count - u[:, :, sl]
            pooled.append(p.reshape(nseq * t_len, POOL_GROUP).astype(jnp.bfloat16))
        st["u"] = u
        st["pooled"] = pooled

    def gate_merge(st):
        h = st.pop("h")
        pooled = st.pop("pooled")
        yb = jnp.concatenate([_dot(p, w_pool[gi]) for gi, p in enumerate(pooled)], axis=-1)
        yb = yb * pool_scale[...]
        bgate = b_gate[...]
        g_conv = jax.nn.sigmoid(_dot(h, w_in[:, 4 * d:5 * d]) + bgate[:, :d])
        g_pool = jax.nn.sigmoid(_dot(h, w_in[:, 5 * d:6 * d]) + bgate[:, d:])
        st["merged"] = (g_conv * st.pop("ya") + g_pool * yb).astype(jnp.bfloat16)

    def out_proj(st):
        x1 = st["x"] + _rmsnorm(_dot(st.pop("merged"), w_o[...]), g_post_mix[...])
        st["x1"] = x1
        st["h2"] = _rmsnorm(x1, g_pre_mlp[...]).astype(jnp.bfloat16)

    def mlp_chunk(c):
        cs = slice(c * FF_CHUNK, (c + 1) * FF_CHUNK)

        def stage(st):
            a = jnp.square(jnp.maximum(_dot(st["h2"], w_up[:, cs]), 0.0))
            part = _dot(a, w_down[cs, :])
            st["f"] = part if c == 0 else st["f"] + part
        return stage

    def post_norm(st):
        st["y"] = st.pop("x1") + _rmsnorm(st.pop("f"), g_post_mlp[...])

    return ([pre_norm, conv_branch, pool_branch, gate_merge, out_proj]
            + [mlp_chunk(c) for c in range(D_FF // FF_CHUNK)] + [post_norm])


def _run_layer(tiles, w):
    for stage in _layer_stages(w):
        for st in tiles:
            stage(st)


def _prompt_kernel(x_ref, *refs):
    w = refs[:13]
    y_ref, conv_ref, pool_ref, z_tail, u_tail = refs[13:]
    j = pl.program_id(1)
    n_seqs, t_len, _ = x_ref.shape

    @pl.when(j == 0)
    def _():
        z_tail[...] = jnp.zeros_like(z_tail)
        u_tail[...] = jnp.zeros_like(u_tail)

    tiles = [dict(x=x_ref[s], z_hist=z_tail[s][None], u_hist=u_tail[s][None],
                  start_pos=j * t_len, nseq=1, t_len=t_len) for s in range(n_seqs)]
    _run_layer(tiles, w)
    for s, st in enumerate(tiles):
        y_ref[s] = st["y"]
        z_last = st["z"][0, t_len - CONV_HIST:, :]
        u_last = st["u"][0, t_len - POOL_HIST:, :]
        z_tail[s] = z_last
        u_tail[s] = u_last
        conv_ref[s] = z_last
        pool_ref[s] = u_last


def _sample_kernel(x_ref, zs_ref, us_ref, *refs):
    w = refs[:13]
    y_ref, conv_ref, pool_ref = refs[13:]
    nseq = us_ref.shape[0]
    t_len = x_ref.shape[0] // nseq
    z_hist = zs_ref[...].reshape(nseq, CONV_HIST, D_MODEL)
    st = dict(x=x_ref[...], z_hist=z_hist, u_hist=us_ref[...], start_pos=PAST_LEN,
              nseq=nseq, t_len=t_len)
    _run_layer([st], w)
    y_ref[...] = st["y"]
    conv_ref[...] = st["z"].reshape(nseq * t_len, D_MODEL)
    pool_ref[...] = st["u"].reshape(nseq * t_len, D_MODEL)


def _weight_specs(n_grid_axes):
    def const(shape):
        zeros = (0,) * len(shape)
        if n_grid_axes == 1:
            index_map = lambda i: zeros
        else:
            index_map = lambda b, j: zeros
        return pl.BlockSpec(shape, index_map, pipeline_mode=pl.Buffered(1))
    d = D_MODEL
    return [
        const((1, d)),
        const((d, 6 * d)),
        const((1, 2 * d)),
        const((CONV_WIDTH, d)),
        const((d, d)),
        const((len(POOL_WINDOWS), POOL_GROUP, POOL_GROUP)),
        const((1, d)),
        const((d, d)),
        const((1, d)),
        const((1, d)),
        const((d, D_FF)),
        const((D_FF, d)),
        const((1, d)),
    ]


def _prompt_call(x, weights):
    batch, seq, d = x.shape
    tm = PROMPT_TILE
    nb = PROMPT_SEQS
    f32 = jnp.float32
    return pl.pallas_call(
        _prompt_kernel,
        grid=(batch // nb, seq // tm),
        in_specs=[pl.BlockSpec((nb, tm, d), lambda b, j: (b, j, 0))] + _weight_specs(2),
        out_specs=[
            pl.BlockSpec((nb, tm, d), lambda b, j: (b, j, 0)),
            pl.BlockSpec((nb, CONV_HIST, d), lambda b, j: (b, 0, 0)),
            pl.BlockSpec((nb, POOL_HIST, d), lambda b, j: (b, 0, 0)),
        ],
        out_shape=[
            jax.ShapeDtypeStruct((batch, seq, d), f32),
            jax.ShapeDtypeStruct((batch, CONV_HIST, d), f32),
            jax.ShapeDtypeStruct((batch, POOL_HIST, d), f32),
        ],
        scratch_shapes=[pltpu.VMEM((nb, CONV_HIST, d), f32),
                        pltpu.VMEM((nb, POOL_HIST, d), f32)],
        compiler_params=pltpu.CompilerParams(
            dimension_semantics=("arbitrary", "arbitrary"),
            vmem_limit_bytes=VMEM_LIMIT_BYTES),
        name="prompt_layer",
    )(x, *weights)


def _sample_call(x2d, z_state, u_state, weights):
    rows, d = x2d.shape
    nseq_total = u_state.shape[0]
    t_len = rows // nseq_total
    ns = SAMPLE_SEQS
    f32 = jnp.float32
    return pl.pallas_call(
        _sample_kernel,
        grid=(nseq_total // ns,),
        in_specs=[
            pl.BlockSpec((ns * t_len, d), lambda i: (i, 0)),
            pl.BlockSpec((ns * CONV_HIST, d), lambda i: (i, 0)),
            pl.BlockSpec((ns, POOL_HIST, d), lambda i: (i, 0, 0)),
        ] + _weight_specs(1),
        out_specs=[
            pl.BlockSpec((ns * t_len, d), lambda i: (i, 0)),
            pl.BlockSpec((ns * t_len, d), lambda i: (i, 0)),
            pl.BlockSpec((ns * t_len, d), lambda i: (i, 0)),
        ],
        out_shape=[jax.ShapeDtypeStruct((rows, d), f32)] * 3,
        compiler_params=pltpu.CompilerParams(
            dimension_semantics=("arbitrary",),
            vmem_limit_bytes=VMEM_LIMIT_BYTES),
        name="sample_layer",
    )(x2d, z_state, u_state, *weights)


def kernel(x_prompt, x_sample, state_conv, state_pool, g_pre_mix, w_in, b_gate, w_conv,
           w_out_conv, w_pool_group, pool_scale, w_o, g_post_mix, g_pre_mlp, w_up, w_down,
           g_post_mlp):
    depth = w_in.shape[0]
    assert depth == 1, "single-layer step"
    bf16 = jnp.bfloat16
    weights = (
        g_pre_mix, w_in[0].astype(bf16), b_gate, w_conv[0], w_out_conv[0].astype(bf16),
        w_pool_group[0].astype(bf16), pool_scale, w_o[0].astype(bf16), g_post_mix,
        g_pre_mlp, w_up[0].astype(bf16), w_down[0].astype(bf16), g_post_mlp)

    y_prompt, conv_tail, pool_tail = _prompt_call(x_prompt, weights)
    new_conv_prompt = conv_tail[None, :, CONV_HIST - (CONV_WIDTH - 1):, :]
    new_pool_prompt = pool_tail[None, :, POOL_HIST - POOL_BUF:, :]

    nseq, t_len, d = x_sample.shape
    z_state = jnp.pad(state_conv[0], ((0, 0), (CONV_HIST - (CONV_WIDTH - 1), 0), (0, 0)))
    u_state = jnp.pad(state_pool[0], ((0, 0), (POOL_HIST - POOL_BUF, 0), (0, 0)))
    y_s, z_s, u_s = _sample_call(
        x_sample.reshape(nseq * t_len, d), z_state.reshape(nseq * CONV_HIST, d), u_state, weights)
    y_sample = y_s.reshape(nseq, t_len, d)
    zc = jnp.concatenate([state_conv[0], z_s.reshape(nseq, t_len, d)], axis=1)
    uc = jnp.concatenate([state_pool[0], u_s.reshape(nseq, t_len, d)], axis=1)
    new_conv_sample = zc[None, :, -(CONV_WIDTH - 1):, :]
    new_pool_sample = uc[None, :, -POOL_BUF:, :]
    return (y_prompt, y_sample, new_conv_prompt, new_pool_prompt, new_conv_sample,
            new_pool_sample)
```

```python
import jax
import jax.numpy as jnp
from jax.experimental import pallas as pl
from jax.experimental.pallas import tpu as pltpu

D_MODEL = 1024
D_FF = 4 * D_MODEL
CONV_WIDTH = 3
POOL_WINDOWS = (2, 4, 8, 16)
POOL_GROUP = D_MODEL // len(POOL_WINDOWS)
POOL_BUF = max(POOL_WINDOWS) - 1
PAST_LEN = 16384
EPS = 1e-6

CONV_HIST = 8
POOL_HIST = 16

PROMPT_TILE = 256
PROMPT_SEQS = 2
SAMPLE_SEQS = 32
FF_CHUNK = 1024
VMEM_LIMIT_BYTES = 58 * 1024 * 1024


def _rmsnorm(x, g):
    r = jax.lax.rsqrt(jnp.mean(x * x, axis=-1, keepdims=True) + EPS)
    return x * r * g


def _dot(a, w):
    return jnp.dot(a.astype(jnp.bfloat16), w, preferred_element_type=jnp.float32)


def _layer_stages(w):
    (g_pre_mix, w_in, b_gate, w_conv, w_out_conv, w_pool, pool_scale, w_o,
     g_post_mix, g_pre_mlp, w_up, w_down, g_post_mlp) = w
    d = D_MODEL

    def pre_norm(st):
        st["h"] = _rmsnorm(st["x"], g_pre_mix[...]).astype(jnp.bfloat16)

    def conv_branch(st):
        nseq, t_len, h = st["nseq"], st["t_len"], st["h"]
        xv = _dot(h, w_in[:, 0 * d:1 * d])
        cg = _dot(h, w_in[:, 2 * d:3 * d])
        bg = _dot(h, w_in[:, 1 * d:2 * d])
        z = (cg * xv).reshape(nseq, t_len, d)
        zext = jnp.concatenate([st["z_hist"], z], axis=1)
        wc = w_conv[...]
        conv = zext * wc[2:3, :][None]
        conv = conv + pltpu.roll(zext, 1, axis=1) * wc[1:2, :][None]
        conv = conv + pltpu.roll(zext, 2, axis=1) * wc[0:1, :][None]
        conv = conv[:, CONV_HIST:, :].reshape(nseq * t_len, d)
        st["z"] = z
        st["yc"] = (bg * conv).astype(jnp.bfloat16)

    def pool_branch(st):
        nseq, t_len, h = st["nseq"], st["t_len"], st["h"]
        st["ya"] = _dot(st.pop("yc"), w_out_conv[...])
        u = _dot(h, w_in[:, 3 * d:4 * d]).reshape(nseq, t_len, d)
        uext = jnp.concatenate([st["u_hist"], u], axis=1)
        pos = st["start_pos"] + jax.lax.broadcasted_iota(jnp.int32, (1, t_len, 1), 1)
        pooled = []
        for gi, win in enumerate(POOL_WINDOWS):
            sl = slice(gi * POOL_GROUP, (gi + 1) * POOL_GROUP)
            s = uext[:, :, sl]
            k = 1
            while k < win:
                s = s + pltpu.roll(s, k, axis=1)
                k *= 2
            inv_count = 1.0 / jnp.minimum(win, pos + 1).astype(jnp.float32)
            p = s[:, POOL_HIST:, :] * inv_count - u[:, :, sl]
            pooled.append(p.reshape(nseq * t_len, POOL_GROUP).astype(jnp.bfloat16))
        st["u"] = u
        st["pooled"] = pooled

    def gate_merge(st):
        h = st.pop("h")
        pooled = st.pop("pooled")
        yb = jnp.concatenate([_dot(p, w_pool[gi]) for gi, p in enumerate(pooled)], axis=-1)
        yb = yb * pool_scale[...]
        bgate = b_gate[...]
        g_conv = jax.nn.sigmoid(_dot(h, w_in[:, 4 * d:5 * d]) + bgate[:, :d])
        g_pool = jax.nn.sigmoid(_dot(h, w_in[:, 5 * d:6 * d]) + bgate[:, d:])
        st["merged"] = (g_conv * st.pop("ya") + g_pool * yb).astype(jnp.bfloat16)

    def out_proj(st):
        x1 = st["x"] + _rmsnorm(_dot(st.pop("merged"), w_o[...]), g_post_mix[...])
        st["x1"] = x1
        st["h2"] = _rmsnorm(x1, g_pre_mlp[...]).astype(jnp.bfloat16)

    def mlp_chunk(c):
        cs = slice(c * FF_CHUNK, (c + 1) * FF_CHUNK)

        def stage(st):
            a = jnp.square(jnp.maximum(_dot(st["h2"], w_up[:, cs]), 0.0))
            part = _dot(a, w_down[cs, :])
            st["f"] = part if c == 0 else st["f"] + part
        return stage

    def post_norm(st):
        st["y"] = st.pop("x1") + _rmsnorm(st.pop("f"), g_post_mlp[...])

    return ([pre_norm, conv_branch, pool_branch, gate_merge, out_proj]
            + [mlp_chunk(c) for c in range(D_FF // FF_CHUNK)] + [post_norm])


def _run_layer(tiles, w):
    stages = _layer_stages(w)
    for step in range(len(stages) + len(tiles) - 1):
        for i, st in enumerate(tiles):
            if 0 <= step - i < len(stages):
                stages[step - i](st)


def _prompt_kernel(x_ref, *refs):
    w = refs[:13]
    y_ref, conv_ref, pool_ref, z_tail, u_tail = refs[13:]
    j = pl.program_id(1)
    n_seqs, t_len, _ = x_ref.shape

    @pl.when(j == 0)
    def _():
        z_tail[...] = jnp.zeros_like(z_tail)
        u_tail[...] = jnp.zeros_like(u_tail)

    tiles = [dict(x=x_ref[s], z_hist=z_tail[s][None], u_hist=u_tail[s][None],
                  start_pos=j * t_len, nseq=1, t_len=t_len) for s in range(n_seqs)]
    _run_layer(tiles, w)
    for s, st in enumerate(tiles):
        y_ref[s] = st["y"]
        z, u = st["z"][0], st["u"][0]
        z_tail[s] = z[t_len - CONV_HIST:, :]
        u_tail[s] = u[t_len - POOL_HIST:, :]
        conv_ref[s] = z[t_len - (CONV_WIDTH - 1):, :]
        pool_ref[s] = u[t_len - POOL_BUF:, :]


def _sample_kernel(x_ref, zs_ref, us_ref, *refs):
    w = refs[:13]
    y_ref, conv_ref, pool_ref = refs[13:]
    nseq = us_ref.shape[0]
    t_len = x_ref.shape[0] // nseq
    f32 = jnp.float32
    zs, us = zs_ref[...], us_ref[...]
    z_hist = jnp.concatenate(
        [jnp.zeros((nseq, CONV_HIST - (CONV_WIDTH - 1), D_MODEL), f32), zs], axis=1)
    u_hist = jnp.concatenate([jnp.zeros((nseq, POOL_HIST - POOL_BUF, D_MODEL), f32), us], axis=1)
    st = dict(x=x_ref[...], z_hist=z_hist, u_hist=u_hist, start_pos=PAST_LEN,
              nseq=nseq, t_len=t_len)
    _run_layer([st], w)
    y_ref[...] = st["y"]
    conv_ref[...] = st["z"][:, t_len - (CONV_WIDTH - 1):, :]
    pool_ref[...] = jnp.concatenate([us[:, t_len:, :], st["u"]], axis=1)


def _weight_specs(n_grid_axes):
    def const(shape):
        zeros = (0,) * len(shape)
        if n_grid_axes == 1:
            index_map = lambda i: zeros
        else:
            index_map = lambda b, j: zeros
        return pl.BlockSpec(shape, index_map, pipeline_mode=pl.Buffered(1))
    d = D_MODEL
    return [
        const((1, d)),
        const((d, 6 * d)),
        const((1, 2 * d)),
        const((CONV_WIDTH, d)),
        const((d, d)),
        const((len(POOL_WINDOWS), POOL_GROUP, POOL_GROUP)),
        const((1, d)),
        const((d, d)),
        const((1, d)),
        const((1, d)),
        const((d, D_FF)),
        const((D_FF, d)),
        const((1, d)),
    ]


def _prompt_call(x, weights):
    batch, seq, d = x.shape
    tm = PROMPT_TILE
    nb = PROMPT_SEQS
    f32 = jnp.float32
    return pl.pallas_call(
        _prompt_kernel,
        grid=(batch // nb, seq // tm),
        in_specs=[pl.BlockSpec((nb, tm, d), lambda b, j: (b, j, 0))] + _weight_specs(2),
        out_specs=[
            pl.BlockSpec((nb, tm, d), lambda b, j: (b, j, 0)),
            pl.BlockSpec((None, nb, CONV_WIDTH - 1, d), lambda b, j: (0, b, 0, 0)),
            pl.BlockSpec((None, nb, POOL_BUF, d), lambda b, j: (0, b, 0, 0)),
        ],
        out_shape=[
            jax.ShapeDtypeStruct((batch, seq, d), f32),
            jax.ShapeDtypeStruct((1, batch, CONV_WIDTH - 1, d), f32),
            jax.ShapeDtypeStruct((1, batch, POOL_BUF, d), f32),
        ],
        scratch_shapes=[pltpu.VMEM((nb, CONV_HIST, d), f32),
                        pltpu.VMEM((nb, POOL_HIST, d), f32)],
        compiler_params=pltpu.CompilerParams(
            dimension_semantics=("arbitrary", "arbitrary"),
            vmem_limit_bytes=VMEM_LIMIT_BYTES),
        name="prompt_layer",
    )(x, *weights)


def _sample_call(x2d, z_state, u_state, weights):
    rows, d = x2d.shape
    nseq_total = u_state.shape[1]
    t_len = rows // nseq_total
    assert CONV_WIDTH - 1 <= t_len <= POOL_BUF
    ns = SAMPLE_SEQS
    f32 = jnp.float32
    return pl.pallas_call(
        _sample_kernel,
        grid=(nseq_total // ns,),
        in_specs=[
            pl.BlockSpec((ns * t_len, d), lambda i: (i, 0)),
            pl.BlockSpec((None, ns, CONV_WIDTH - 1, d), lambda i: (0, i, 0, 0)),
            pl.BlockSpec((None, ns, POOL_BUF, d), lambda i: (0, i, 0, 0)),
        ] + _weight_specs(1),
        out_specs=[
            pl.BlockSpec((ns * t_len, d), lambda i: (i, 0)),
            pl.BlockSpec((None, ns, CONV_WIDTH - 1, d), lambda i: (0, i, 0, 0)),
            pl.BlockSpec((None, ns, POOL_BUF, d), lambda i: (0, i, 0, 0)),
        ],
        out_shape=[
            jax.ShapeDtypeStruct((rows, d), f32),
            jax.ShapeDtypeStruct((1, nseq_total, CONV_WIDTH - 1, d), f32),
            jax.ShapeDtypeStruct((1, nseq_total, POOL_BUF, d), f32),
        ],
        compiler_params=pltpu.CompilerParams(
            dimension_semantics=("arbitrary",),
            vmem_limit_bytes=VMEM_LIMIT_BYTES),
        name="sample_layer",
    )(x2d, z_state, u_state, *weights)


def kernel(x_prompt, x_sample, state_conv, state_pool, g_pre_mix, w_in, b_gate, w_conv,
           w_out_conv, w_pool_group, pool_scale, w_o, g_post_mix, g_pre_mlp, w_up, w_down,
           g_post_mlp):
    depth = w_in.shape[0]
    assert depth == 1, "single-layer step"
    bf16 = jnp.bfloat16
    weights = (
        g_pre_mix, w_in[0].astype(bf16), b_gate, w_conv[0], w_out_conv[0].astype(bf16),
        w_pool_group[0].astype(bf16), pool_scale, w_o[0].astype(bf16), g_post_mix,
        g_pre_mlp, w_up[0].astype(bf16), w_down[0].astype(bf16), g_post_mlp)

    y_prompt, new_conv_prompt, new_pool_prompt = _prompt_call(x_prompt, weights)

    nseq, t_len, d = x_sample.shape
    y_s, new_conv_sample, new_pool_sample = _sample_call(
        x_sample.reshape(nseq * t_len, d), state_conv, state_pool, weights)
    y_sample = y_s.reshape(nseq, t_len, d)
    return (y_prompt, y_sample, new_conv_prompt, new_pool_prompt, new_conv_sample,
            new_pool_sample)
```

```python
import jax
import jax.numpy as jnp
from jax.experimental import pallas as pl
from jax.experimental.pallas import tpu as pltpu

D_MODEL = 1024
D_FF = 4 * D_MODEL
CONV_WIDTH = 3
POOL_WINDOWS = (2, 4, 8, 16)
POOL_GROUP = D_MODEL // len(POOL_WINDOWS)
POOL_BUF = max(POOL_WINDOWS) - 1
PAST_LEN = 16384
EPS = 1e-6

CONV_HIST = 8
POOL_HIST = 16

PROMPT_TILE = 256
PROMPT_SEQS = 2
SAMPLE_SEQS = 32
FF_CHUNK = 1024
VMEM_LIMIT_BYTES = 58 * 1024 * 1024

MATMUL_WEIGHTS = (
    ("w_in", (D_MODEL, 6 * D_MODEL), 64),
    ("w_out_conv", (D_MODEL, D_MODEL), 256),
    ("w_pool_group", (len(POOL_WINDOWS) * POOL_GROUP, POOL_GROUP), POOL_GROUP),
    ("w_o", (D_MODEL, D_MODEL), 256),
    ("w_up", (D_MODEL, D_FF), 64),
    ("w_down", (D_FF, D_MODEL), 256),
)


def _rmsnorm(x, g):
    r = jax.lax.rsqrt(jnp.mean(x * x, axis=-1, keepdims=True) + EPS)
    return x * r * g


def _dot(a, w):
    return jnp.dot(a.astype(jnp.bfloat16), w, preferred_element_type=jnp.float32)


def _layer_stages(w):
    (g_pre_mix, w_in, b_gate, w_conv, w_out_conv, w_pool, pool_scale, w_o,
     g_post_mix, g_pre_mlp, w_up, w_down, g_post_mlp) = w
    d = D_MODEL

    def pre_norm(st):
        st["h"] = _rmsnorm(st["x"], g_pre_mix[...]).astype(jnp.bfloat16)

    def conv_branch(st):
        h = st["h"]
        xv = _dot(h, w_in[:, 0 * d:1 * d])
        cg = _dot(h, w_in[:, 2 * d:3 * d])
        bg = _dot(h, w_in[:, 1 * d:2 * d])
        z = cg * xv
        rows = z.shape[0]
        zext = jnp.concatenate([st["z_hist"], z], axis=0)
        taps = [w_conv[:, k * d:(k + 1) * d] for k in range(CONV_WIDTH)]
        if st["time_major"]:
            ns = st["nseq"]
            conv = zext[0:rows] * taps[0]
            for k in range(1, CONV_WIDTH):
                conv = conv + zext[k * ns:k * ns + rows] * taps[k]
        else:
            conv = zext * taps[CONV_WIDTH - 1]
            for k in range(1, CONV_WIDTH):
                conv = conv + pltpu.roll(zext, k, axis=0) * taps[CONV_WIDTH - 1 - k]
            conv = conv[CONV_HIST:]
        st["z"] = z
        st["yc"] = (bg * conv).astype(jnp.bfloat16)

    def pool_branch(st):
        h = st["h"]
        st["ya"] = _dot(st.pop("yc"), w_out_conv[...])
        u = _dot(h, w_in[:, 3 * d:4 * d])
        rows = u.shape[0]
        uext = jnp.concatenate([st["u_hist"], u], axis=0)
        row = jax.lax.broadcasted_iota(jnp.int32, (rows, 1), 0)
        ns = st["nseq"]
        assert ns & (ns - 1) == 0, "time-major row -> time uses a shift"
        pos = st["start_pos"] + (row >> (ns.bit_length() - 1) if st["time_major"] else row)
        pooled = []
        for gi, win in enumerate(POOL_WINDOWS):
            sl = slice(gi * POOL_GROUP, (gi + 1) * POOL_GROUP)
            s = uext[:, sl]
            k = 1
            if st["time_major"]:
                lost = 0
                while k < win:
                    n = s.shape[0]
                    s = s[k * ns:] + s[:n - k * ns]
                    lost += k
                    k *= 2
                win_sum = s[(POOL_BUF - lost) * ns:]
            else:
                while k < win:
                    s = s + pltpu.roll(s, k, axis=0)
                    k *= 2
                win_sum = s[POOL_HIST:]
            inv_count = 1.0 / jnp.minimum(win, pos + 1).astype(jnp.float32)
            pooled.append((win_sum * inv_count - u[:, sl]).astype(jnp.bfloat16))
        st["uext"] = uext
        st["pooled"] = pooled

    def gate_merge(st):
        h = st.pop("h")
        pooled = st.pop("pooled")
        yb = jnp.concatenate(
            [_dot(p, w_pool[gi * POOL_GROUP:(gi + 1) * POOL_GROUP, :])
             for gi, p in enumerate(pooled)], axis=-1)
        yb = yb * pool_scale[...]
        bgate = b_gate[...]
        g_conv = jax.nn.sigmoid(_dot(h, w_in[:, 4 * d:5 * d]) + bgate[:, :d])
        g_pool = jax.nn.sigmoid(_dot(h, w_in[:, 5 * d:6 * d]) + bgate[:, d:])
        st["merged"] = (g_conv * st.pop("ya") + g_pool * yb).astype(jnp.bfloat16)

    def out_proj(st):
        x1 = st["x"] + _rmsnorm(_dot(st.pop("merged"), w_o[...]), g_post_mix[...])
        st["x1"] = x1
        st["h2"] = _rmsnorm(x1, g_pre_mlp[...]).astype(jnp.bfloat16)

    def mlp_chunk(c):
        cs = slice(c * FF_CHUNK, (c + 1) * FF_CHUNK)

        def stage(st):
            a = jnp.square(jnp.maximum(_dot(st["h2"], w_up[:, cs]), 0.0))
            part = _dot(a, w_down[cs, :])
            st["f"] = part if c == 0 else st["f"] + part
        return stage

    def post_norm(st):
        st["y"] = st.pop("x1") + _rmsnorm(st.pop("f"), g_post_mlp[...])

    return ([pre_norm, conv_branch, pool_branch, gate_merge, out_proj]
            + [mlp_chunk(c) for c in range(D_FF // FF_CHUNK)] + [post_norm])


def _run_layer(tiles, w):
    stages = _layer_stages(w)
    for step in range(len(stages) + len(tiles) - 1):
        for i, st in enumerate(tiles):
            if 0 <= step - i < len(stages):
                stages[step - i](st)


def _prompt_kernel(x_ref, *refs):
    w = refs[:13]
    y_ref, conv_ref, pool_ref, z_tail, u_tail = refs[13:]
    j = pl.program_id(1)
    n_seqs, t_len, _ = x_ref.shape

    @pl.when(j == 0)
    def _():
        z_tail[...] = jnp.zeros_like(z_tail)
        u_tail[...] = jnp.zeros_like(u_tail)

    tiles = [dict(x=x_ref[s], z_hist=z_tail[s], u_hist=u_tail[s], time_major=False,
                  start_pos=j * t_len, nseq=1) for s in range(n_seqs)]
    _run_layer(tiles, w)
    for s, st in enumerate(tiles):
        y_ref[s] = st["y"]
        z, uext = st["z"], st["uext"]
        z_tail[s] = z[t_len - CONV_HIST:, :]
        u_tail[s] = uext[t_len:, :]
        conv_ref[s] = z[t_len - (CONV_WIDTH - 1):, :]
        pool_ref[s] = uext[POOL_HIST + t_len - POOL_BUF:, :]


def _weight_copy_jobs(w_hbm, stage_bufs, in_sems):
    jobs = []
    used = {}
    for k, (_, (n_rows, _), chunk) in enumerate(MATMUL_WEIGHTS):
        stage = stage_bufs[k]
        sem = in_sems[k]
        for r0 in range(0, n_rows, chunk):
            slot = used.get(id(stage), 0) % 2
            used[id(stage)] = used.get(id(stage), 0) + 1
            src = w_hbm[k].at[pl.ds(r0, chunk), :]
            dst = stage.at[slot]
            jobs.append((pltpu.make_async_copy(src, dst, sem.at[slot]), k, r0, dst))
    return jobs


def _convert_weights(w_hbm, w_bf16, stage_bufs, in_sems):
    jobs = _weight_copy_jobs(w_hbm, stage_bufs, in_sems)
    jobs[0][0].start()
    for n, (copy, k, r0, staged) in enumerate(jobs):
        if n + 1 < len(jobs):
            jobs[n + 1][0].start()
        copy.wait()
        chunk = MATMUL_WEIGHTS[k][2]
        w_bf16[k][pl.ds(r0, chunk), :] = staged[...].astype(jnp.bfloat16)


def _sample_kernel(x_ref, zs_ref, us_ref, g_pre_mix, b_gate, w_conv, pool_scale, g_post_mix,
                   g_pre_mlp, g_post_mlp, *refs):
    n_w = len(MATMUL_WEIGHTS)
    w_hbm = refs[:n_w]
    y_ref, conv_ref, pool_ref = refs[n_w:n_w + 3]
    w_out_hbm = refs[n_w + 3:2 * n_w + 3]
    w_bf16 = refs[2 * n_w + 3:3 * n_w + 3]
    stage_a, stage_b, stage_c, stage_e, sem_a, sem_b, sem_c, sem_e, out_sems = refs[3 * n_w + 3:]
    i = pl.program_id(0)
    w_in, w_out_conv, w_pool, w_o, w_up, w_down = w_bf16

    def out_copy(k):
        return pltpu.make_async_copy(w_bf16[k], w_out_hbm[k], out_sems.at[k])

    @pl.when(i == 0)
    def _():
        stage_bufs = (stage_a, stage_b, stage_c, stage_b, stage_e, stage_b)
        in_sems = (sem_a, sem_b, sem_c, sem_b, sem_e, sem_b)
        _convert_weights(w_hbm, w_bf16, stage_bufs, in_sems)
        for k in range(n_w):
            out_copy(k).start()

    nseq = us_ref.shape[1]
    t_len = x_ref.shape[1]
    x = jnp.concatenate([x_ref[:, t, :] for t in range(t_len)], axis=0)
    z_hist = jnp.concatenate([zs_ref[:, k, :] for k in range(CONV_WIDTH - 1)], axis=0)
    u_hist = us_ref[...].reshape(POOL_BUF * nseq, D_MODEL)
    st = dict(x=x, z_hist=z_hist, u_hist=u_hist, time_major=True, start_pos=PAST_LEN, nseq=nseq)
    w = (g_pre_mix, w_in, b_gate, w_conv, w_out_conv, w_pool, pool_scale, w_o,
         g_post_mix, g_pre_mlp, w_up, w_down, g_post_mlp)
    _run_layer([st], w)
    y, z, uext = st["y"], st["z"], st["uext"]
    for t in range(t_len):
        y_ref[:, t, :] = y[t * nseq:(t + 1) * nseq]
    for k in range(CONV_WIDTH - 1):
        t = t_len - (CONV_WIDTH - 1) + k
        conv_ref[:, k, :] = z[t * nseq:(t + 1) * nseq]
    pool_ref[...] = uext[t_len * nseq:].reshape(POOL_BUF, nseq, D_MODEL)

    @pl.when(i == pl.num_programs(0) - 1)
    def _():
        for k in range(n_w):
            out_copy(k).wait()


def _const_spec(shape, n_grid_axes):
    zeros = (0,) * len(shape)
    if n_grid_axes == 1:
        index_map = lambda i: zeros
    else:
        index_map = lambda b, j: zeros
    return pl.BlockSpec(shape, index_map, pipeline_mode=pl.Buffered(1))


def _prompt_call(x, weights):
    batch, seq, d = x.shape
    tm = PROMPT_TILE
    nb = PROMPT_SEQS
    f32 = jnp.float32
    return pl.pallas_call(
        _prompt_kernel,
        grid=(batch // nb, seq // tm),
        in_specs=([pl.BlockSpec((nb, tm, d), lambda b, j: (b, j, 0))]
                  + [_const_spec(a.shape, 2) for a in weights]),
        out_specs=[
            pl.BlockSpec((nb, tm, d), lambda b, j: (b, j, 0)),
            pl.BlockSpec((None, nb, CONV_WIDTH - 1, d), lambda b, j: (0, b, 0, 0)),
            pl.BlockSpec((None, nb, POOL_BUF, d), lambda b, j: (0, b, 0, 0)),
        ],
        out_shape=[
            jax.ShapeDtypeStruct((batch, seq, d), f32),
            jax.ShapeDtypeStruct((1, batch, CONV_WIDTH - 1, d), f32),
            jax.ShapeDtypeStruct((1, batch, POOL_BUF, d), f32),
        ],
        scratch_shapes=[pltpu.VMEM((nb, CONV_HIST, d), f32),
                        pltpu.VMEM((nb, POOL_HIST, d), f32)],
        compiler_params=pltpu.CompilerParams(
            dimension_semantics=("arbitrary", "arbitrary"),
            vmem_limit_bytes=VMEM_LIMIT_BYTES),
        name="prompt_layer",
    )(x, *weights)


def _sample_call(x, state_conv, state_pool_tm, vectors, w_f32):
    nseq_total, t_len, d = x.shape
    assert CONV_WIDTH - 1 <= t_len <= POOL_BUF
    ns = SAMPLE_SEQS
    f32, bf16 = jnp.float32, jnp.bfloat16
    any_spec = pl.BlockSpec(memory_space=pl.ANY)
    w_shapes = [shape for _, shape, _ in MATMUL_WEIGHTS]
    stage_shapes = [(2, MATMUL_WEIGHTS[k][2], MATMUL_WEIGHTS[k][1][1]) for k in (0, 1, 2, 4)]
    return pl.pallas_call(
        _sample_kernel,
        grid=(nseq_total // ns,),
        in_specs=([
            pl.BlockSpec((ns, t_len, d), lambda i: (i, 0, 0)),
            pl.BlockSpec((None, ns, CONV_WIDTH - 1, d), lambda i: (0, i, 0, 0)),
            pl.BlockSpec((POOL_BUF, ns, d), lambda i: (0, i, 0)),
        ] + [_const_spec(v.shape, 1) for v in vectors] + [any_spec] * len(w_f32)),
        out_specs=[
            pl.BlockSpec((ns, t_len, d), lambda i: (i, 0, 0)),
            pl.BlockSpec((None, ns, CONV_WIDTH - 1, d), lambda i: (0, i, 0, 0)),
            pl.BlockSpec((POOL_BUF, ns, d), lambda i: (0, i, 0)),
        ] + [any_spec] * len(w_f32),
        out_shape=[
            jax.ShapeDtypeStruct((nseq_total, t_len, d), f32),
            jax.ShapeDtypeStruct((1, nseq_total, CONV_WIDTH - 1, d), f32),
            jax.ShapeDtypeStruct((POOL_BUF, nseq_total, d), f32),
        ] + [jax.ShapeDtypeStruct(s, bf16) for s in w_shapes],
        scratch_shapes=([pltpu.VMEM(s, bf16) for s in w_shapes]
                        + [pltpu.VMEM(s, f32) for s in stage_shapes]
                        + [pltpu.SemaphoreType.DMA((2,))] * len(stage_shapes)
                        + [pltpu.SemaphoreType.DMA((len(w_f32),))]),
        compiler_params=pltpu.CompilerParams(
            dimension_semantics=("arbitrary",),
            vmem_limit_bytes=VMEM_LIMIT_BYTES),
        name="sample_layer",
    )(x, state_conv, state_pool_tm, *vectors, *w_f32)


def kernel(x_prompt, x_sample, state_conv, state_pool, g_pre_mix, w_in, b_gate, w_conv,
           w_out_conv, w_pool_group, pool_scale, w_o, g_post_mix, g_pre_mlp, w_up, w_down,
           g_post_mlp):
    depth = w_in.shape[0]
    assert depth == 1, "single-layer step"
    d = D_MODEL
    w_conv_flat = w_conv.reshape(1, CONV_WIDTH * d)
    vectors = (g_pre_mix, b_gate, w_conv_flat, pool_scale, g_post_mix, g_pre_mlp, g_post_mlp)
    w_f32 = (w_in[0], w_out_conv[0], w_pool_group[0].reshape(MATMUL_WEIGHTS[2][1]), w_o[0],
             w_up[0], w_down[0])

    state_pool_tm = jnp.transpose(state_pool[0], (1, 0, 2))
    y_sample, new_conv_sample, pool_tm, *w_bf16 = _sample_call(
        x_sample, state_conv, state_pool_tm, vectors, w_f32)
    new_pool_sample = jnp.transpose(pool_tm, (1, 0, 2))[None]

    wb_in, wb_out_conv, wb_pool, wb_o, wb_up, wb_down = w_bf16
    weights = (g_pre_mix, wb_in, b_gate, w_conv_flat, wb_out_conv, wb_pool, pool_scale, wb_o,
               g_post_mix, g_pre_mlp, wb_up, wb_down, g_post_mlp)
    y_prompt, new_conv_prompt, new_pool_prompt = _prompt_call(x_prompt, weights)
    return (y_prompt, y_sample, new_conv_prompt, new_pool_prompt, new_conv_sample,
            new_pool_sample)
```

```python
import jax
import jax.numpy as jnp
from jax.experimental import pallas as pl
from jax.experimental.pallas import tpu as pltpu

D_MODEL = 1024
D_FF = 4 * D_MODEL
CONV_WIDTH = 3
POOL_WINDOWS = (2, 4, 8, 16)
POOL_GROUP = D_MODEL // len(POOL_WINDOWS)
POOL_BUF = max(POOL_WINDOWS) - 1
PAST_LEN = 16384
EPS = 1e-6

CONV_HIST = 8
POOL_HIST = 16

PROMPT_TILE = 256
PROMPT_SEQS = 2
SAMPLE_SEQS = 32
FF_CHUNK = 1024
VMEM_LIMIT_BYTES = 58 * 1024 * 1024

MATMUL_WEIGHTS = (
    ("w_in", (D_MODEL, 6 * D_MODEL)),
    ("w_out_conv", (D_MODEL, D_MODEL)),
    ("w_pool_group", (len(POOL_WINDOWS) * POOL_GROUP, POOL_GROUP)),
    ("w_o", (D_MODEL, D_MODEL)),
    ("w_up", (D_MODEL, D_FF)),
    ("w_down", (D_FF, D_MODEL)),
)
STAGE_ROWS = 256
STAGE_COLS = 1024
STAGE_SLOTS = 6


def _rmsnorm(x, g):
    r = jax.lax.rsqrt(jnp.mean(x * x, axis=-1, keepdims=True) + EPS)
    return x * r * g


def _dot(a, w):
    return jnp.dot(a.astype(jnp.bfloat16), w, preferred_element_type=jnp.float32)


def _layer_stages(w):
    (g_pre_mix, w_in, b_gate, w_conv, w_out_conv, w_pool, pool_scale, w_o,
     g_post_mix, g_pre_mlp, w_up, w_down, g_post_mlp) = w
    d = D_MODEL

    def pre_norm(st):
        st["h"] = _rmsnorm(st["x"], g_pre_mix[...]).astype(jnp.bfloat16)

    def conv_branch(st):
        h = st["h"]
        xv = _dot(h, w_in[:, 0 * d:1 * d])
        cg = _dot(h, w_in[:, 2 * d:3 * d])
        bg = _dot(h, w_in[:, 1 * d:2 * d])
        z = cg * xv
        rows = z.shape[0]
        zext = jnp.concatenate([st["z_hist"], z], axis=0)
        taps = [w_conv[:, k * d:(k + 1) * d] for k in range(CONV_WIDTH)]
        if st["time_major"]:
            ns = st["nseq"]
            conv = zext[0:rows] * taps[0]
            for k in range(1, CONV_WIDTH):
                conv = conv + zext[k * ns:k * ns + rows] * taps[k]
        else:
            conv = zext * taps[CONV_WIDTH - 1]
            for k in range(1, CONV_WIDTH):
                conv = conv + pltpu.roll(zext, k, axis=0) * taps[CONV_WIDTH - 1 - k]
            conv = conv[CONV_HIST:]
        st["emit_z"](z)
        st["yc"] = (bg * conv).astype(jnp.bfloat16)

    def pool_branch(st):
        h = st["h"]
        st["ya"] = _dot(st.pop("yc"), w_out_conv[...])
        u = _dot(h, w_in[:, 3 * d:4 * d])
        rows = u.shape[0]
        uext = jnp.concatenate([st["u_hist"], u], axis=0)
        row = jax.lax.broadcasted_iota(jnp.int32, (rows, 1), 0)
        ns = st["nseq"]
        assert ns & (ns - 1) == 0, "time-major row -> time uses a shift"
        pos = st["start_pos"] + (row >> (ns.bit_length() - 1) if st["time_major"] else row)
        pooled = []
        for gi, win in enumerate(POOL_WINDOWS):
            sl = slice(gi * POOL_GROUP, (gi + 1) * POOL_GROUP)
            s = uext[:, sl]
            k = 1
            if st["time_major"]:
                lost = 0
                while k < win:
                    n = s.shape[0]
                    s = s[k * ns:] + s[:n - k * ns]
                    lost += k
                    k *= 2
                win_sum = s[(POOL_BUF - lost) * ns:]
            else:
                while k < win:
                    s = s + pltpu.roll(s, k, axis=0)
                    k *= 2
                win_sum = s[POOL_HIST:]
            inv_count = 1.0 / jnp.minimum(win, pos + 1).astype(jnp.float32)
            pooled.append((win_sum * inv_count - u[:, sl]).astype(jnp.bfloat16))
        st["emit_uext"](uext)
        st["pooled"] = pooled

    def gate_merge(st):
        h = st.pop("h")
        pooled = st.pop("pooled")
        yb = jnp.concatenate(
            [_dot(p, w_pool[gi * POOL_GROUP:(gi + 1) * POOL_GROUP, :])
             for gi, p in enumerate(pooled)], axis=-1)
        yb = yb * pool_scale[...]
        bgate = b_gate[...]
        g_conv = jax.nn.sigmoid(_dot(h, w_in[:, 4 * d:5 * d]) + bgate[:, :d])
        g_pool = jax.nn.sigmoid(_dot(h, w_in[:, 5 * d:6 * d]) + bgate[:, d:])
        st["merged"] = (g_conv * st.pop("ya") + g_pool * yb).astype(jnp.bfloat16)

    def out_proj(st):
        x1 = st["x"] + _rmsnorm(_dot(st.pop("merged"), w_o[...]), g_post_mix[...])
        st["x1"] = x1
        st["h2"] = _rmsnorm(x1, g_pre_mlp[...]).astype(jnp.bfloat16)

    def mlp_chunk(c):
        cs = slice(c * FF_CHUNK, (c + 1) * FF_CHUNK)

        def stage(st):
            a = jnp.square(jnp.maximum(_dot(st["h2"], w_up[:, cs]), 0.0))
            part = _dot(a, w_down[cs, :])
            st["f"] = part if c == 0 else st["f"] + part
        return stage

    def post_norm(st):
        st["y"] = st.pop("x1") + _rmsnorm(st.pop("f"), g_post_mlp[...])

    return ([pre_norm, conv_branch, pool_branch, gate_merge, out_proj]
            + [mlp_chunk(c) for c in range(D_FF // FF_CHUNK)] + [post_norm])


def _run_layer(tiles, w):
    stages = _layer_stages(w)
    for step in range(len(stages) + len(tiles) - 1):
        for i, st in enumerate(tiles):
            if 0 <= step - i < len(stages):
                stages[step - i](st)


def _prompt_kernel(x_ref, *refs):
    w = refs[:13]
    y_ref, conv_ref, pool_ref, z_tail, u_tail = refs[13:]
    j = pl.program_id(1)
    n_seqs, t_len, _ = x_ref.shape

    @pl.when(j == 0)
    def _():
        z_tail[...] = jnp.zeros_like(z_tail)
        u_tail[...] = jnp.zeros_like(u_tail)

    kept = [{} for _ in range(n_seqs)]
    tiles = [dict(x=x_ref[s], z_hist=z_tail[s], u_hist=u_tail[s], time_major=False,
                  start_pos=j * t_len, nseq=1,
                  emit_z=lambda z, s=s: kept[s].update(z=z),
                  emit_uext=lambda uext, s=s: kept[s].update(uext=uext))
             for s in range(n_seqs)]
    _run_layer(tiles, w)
    for s, st in enumerate(tiles):
        y_ref[s] = st["y"]
        z, uext = kept[s]["z"], kept[s]["uext"]
        z_tail[s] = z[t_len - CONV_HIST:, :]
        u_tail[s] = uext[t_len:, :]
        conv_ref[s] = z[t_len - (CONV_WIDTH - 1):, :]
        pool_ref[s] = uext[POOL_HIST + t_len - POOL_BUF:, :]


def _weight_copy_jobs(w_hbm, w_bf16, stage, stage_sems, narrow_stage, narrow_sem):
    jobs = []
    n_ring = 0
    for k, (_, (n_rows, n_cols)) in enumerate(MATMUL_WEIGHTS):
        if n_cols % STAGE_COLS:
            assert narrow_stage.shape == (n_rows, n_cols)
            copy = pltpu.make_async_copy(w_hbm[k], narrow_stage, narrow_sem.at[0])
            jobs.append((copy, w_bf16[k], narrow_stage))
            continue
        for r0 in range(0, n_rows, STAGE_ROWS):
            for c0 in range(0, n_cols, STAGE_COLS):
                slot = n_ring % STAGE_SLOTS
                n_ring += 1
                window = (pl.ds(r0, STAGE_ROWS), pl.ds(c0, STAGE_COLS))
                copy = pltpu.make_async_copy(w_hbm[k].at[window], stage.at[slot],
                                             stage_sems.at[slot])
                jobs.append((copy, w_bf16[k].at[window], stage.at[slot]))
    return jobs


def _convert_weights(jobs):
    ahead = STAGE_SLOTS - 1
    for copy, _, _ in jobs[:ahead]:
        copy.start()
    for n, (copy, dst, staged) in enumerate(jobs):
        if n + ahead < len(jobs):
            jobs[n + ahead][0].start()
        copy.wait()
        dst[...] = staged[...].astype(jnp.bfloat16)


def _sample_kernel(x_ref, zs_ref, us_ref, g_pre_mix, b_gate, w_conv, pool_scale, g_post_mix,
                   g_pre_mlp, g_post_mlp, *refs):
    n_w = len(MATMUL_WEIGHTS)
    w_hbm = refs[:n_w]
    y_ref, conv_ref, pool_ref = refs[n_w:n_w + 3]
    w_out_hbm = refs[n_w + 3:2 * n_w + 3]
    w_bf16 = refs[2 * n_w + 3:3 * n_w + 3]
    stage, narrow_stage, stage_sems, narrow_sem, out_sems = refs[3 * n_w + 3:]
    i = pl.program_id(0)
    w_in, w_out_conv, w_pool, w_o, w_up, w_down = w_bf16

    def out_copy(k):
        return pltpu.make_async_copy(w_bf16[k], w_out_hbm[k], out_sems.at[k])

    @pl.when(i == 0)
    def _():
        _convert_weights(
            _weight_copy_jobs(w_hbm, w_bf16, stage, stage_sems, narrow_stage, narrow_sem))
        for k in range(n_w):
            out_copy(k).start()

    nseq = us_ref.shape[1]
    t_len = x_ref.shape[1]
    x = jnp.concatenate([x_ref[:, t, :] for t in range(t_len)], axis=0)
    z_hist = jnp.concatenate([zs_ref[:, k, :] for k in range(CONV_WIDTH - 1)], axis=0)
    u_hist = us_ref[...].reshape(POOL_BUF * nseq, D_MODEL)

    def emit_z(z):
        for k in range(CONV_WIDTH - 1):
            t = t_len - (CONV_WIDTH - 1) + k
            conv_ref[:, k, :] = z[t * nseq:(t + 1) * nseq]

    def emit_uext(uext):
        pool_ref[...] = uext[t_len * nseq:].reshape(POOL_BUF, nseq, D_MODEL)

    st = dict(x=x, z_hist=z_hist, u_hist=u_hist, time_major=True, start_pos=PAST_LEN, nseq=nseq,
              emit_z=emit_z, emit_uext=emit_uext)
    w = (g_pre_mix, w_in, b_gate, w_conv, w_out_conv, w_pool, pool_scale, w_o,
         g_post_mix, g_pre_mlp, w_up, w_down, g_post_mlp)
    _run_layer([st], w)
    y = st["y"]
    for t in range(t_len):
        y_ref[:, t, :] = y[t * nseq:(t + 1) * nseq]

    @pl.when(i == pl.num_programs(0) - 1)
    def _():
        for k in range(n_w):
            out_copy(k).wait()


def _const_spec(shape, n_grid_axes):
    zeros = (0,) * len(shape)
    if n_grid_axes == 1:
        index_map = lambda i: zeros
    else:
        index_map = lambda b, j: zeros
    return pl.BlockSpec(shape, index_map, pipeline_mode=pl.Buffered(1))


def _prompt_call(x, weights):
    batch, seq, d = x.shape
    tm = PROMPT_TILE
    nb = PROMPT_SEQS
    f32 = jnp.float32
    return pl.pallas_call(
        _prompt_kernel,
        grid=(batch // nb, seq // tm),
        in_specs=([pl.BlockSpec((nb, tm, d), lambda b, j: (b, j, 0))]
                  + [_const_spec(a.shape, 2) for a in weights]),
        out_specs=[
            pl.BlockSpec((nb, tm, d), lambda b, j: (b, j, 0)),
            pl.BlockSpec((None, nb, CONV_WIDTH - 1, d), lambda b, j: (0, b, 0, 0)),
            pl.BlockSpec((None, nb, POOL_BUF, d), lambda b, j: (0, b, 0, 0)),
        ],
        out_shape=[
            jax.ShapeDtypeStruct((batch, seq, d), f32),
            jax.ShapeDtypeStruct((1, batch, CONV_WIDTH - 1, d), f32),
            jax.ShapeDtypeStruct((1, batch, POOL_BUF, d), f32),
        ],
        scratch_shapes=[pltpu.VMEM((nb, CONV_HIST, d), f32),
                        pltpu.VMEM((nb, POOL_HIST, d), f32)],
        compiler_params=pltpu.CompilerParams(
            dimension_semantics=("arbitrary", "arbitrary"),
            vmem_limit_bytes=VMEM_LIMIT_BYTES),
        name="prompt_layer",
    )(x, *weights)


def _sample_call(x, state_conv, state_pool_tm, vectors, w_f32):
    nseq_total, t_len, d = x.shape
    assert CONV_WIDTH - 1 <= t_len <= POOL_BUF
    ns = SAMPLE_SEQS
    f32, bf16 = jnp.float32, jnp.bfloat16
    any_spec = pl.BlockSpec(memory_space=pl.ANY)
    w_shapes = [shape for _, shape in MATMUL_WEIGHTS]
    narrow_shapes = [shape for shape in w_shapes if shape[1] % STAGE_COLS]
    assert len(narrow_shapes) == 1
    return pl.pallas_call(
        _sample_kernel,
        grid=(nseq_total // ns,),
        in_specs=([
            pl.BlockSpec((ns, t_len, d), lambda i: (i, 0, 0)),
            pl.BlockSpec((None, ns, CONV_WIDTH - 1, d), lambda i: (0, i, 0, 0)),
            pl.BlockSpec((POOL_BUF, ns, d), lambda i: (0, i, 0)),
        ] + [_const_spec(v.shape, 1) for v in vectors] + [any_spec] * len(w_f32)),
        out_specs=[
            pl.BlockSpec((ns, t_len, d), lambda i: (i, 0, 0)),
            pl.BlockSpec((None, ns, CONV_WIDTH - 1, d), lambda i: (0, i, 0, 0)),
            pl.BlockSpec((POOL_BUF, ns, d), lambda i: (0, i, 0)),
        ] + [any_spec] * len(w_f32),
        out_shape=[
            jax.ShapeDtypeStruct((nseq_total, t_len, d), f32),
            jax.ShapeDtypeStruct((1, nseq_total, CONV_WIDTH - 1, d), f32),
            jax.ShapeDtypeStruct((POOL_BUF, nseq_total, d), f32),
        ] + [jax.ShapeDtypeStruct(s, bf16) for s in w_shapes],
        scratch_shapes=([pltpu.VMEM(s, bf16) for s in w_shapes]
                        + [pltpu.VMEM((STAGE_SLOTS, STAGE_ROWS, STAGE_COLS), f32),
                           pltpu.VMEM(narrow_shapes[0], f32),
                           pltpu.SemaphoreType.DMA((STAGE_SLOTS,)),
                           pltpu.SemaphoreType.DMA((1,)),
                           pltpu.SemaphoreType.DMA((len(w_f32),))]),
        compiler_params=pltpu.CompilerParams(
            dimension_semantics=("arbitrary",),
            vmem_limit_bytes=VMEM_LIMIT_BYTES),
        name="sample_layer",
    )(x, state_conv, state_pool_tm, *vectors, *w_f32)


def kernel(x_prompt, x_sample, state_conv, state_pool, g_pre_mix, w_in, b_gate, w_conv,
           w_out_conv, w_pool_group, pool_scale, w_o, g_post_mix, g_pre_mlp, w_up, w_down,
           g_post_mlp):
    depth = w_in.shape[0]
    assert depth == 1, "single-layer step"
    d = D_MODEL
    w_conv_flat = w_conv.reshape(1, CONV_WIDTH * d)
    vectors = (g_pre_mix, b_gate, w_conv_flat, pool_scale, g_post_mix, g_pre_mlp, g_post_mlp)
    w_f32 = (w_in[0], w_out_conv[0], w_pool_group[0].reshape(MATMUL_WEIGHTS[2][1]), w_o[0],
             w_up[0], w_down[0])

    state_pool_tm = jnp.transpose(state_pool[0], (1, 0, 2))
    y_sample, new_conv_sample, pool_tm, *w_bf16 = _sample_call(
        x_sample, state_conv, state_pool_tm, vectors, w_f32)
    new_pool_sample = jnp.transpose(pool_tm, (1, 0, 2))[None]

    wb_in, wb_out_conv, wb_pool, wb_o, wb_up, wb_down = w_bf16
    weights = (g_pre_mix, wb_in, b_gate, w_conv_flat, wb_out_conv, wb_pool, pool_scale, wb_o,
               g_post_mix, g_pre_mlp, wb_up, wb_down, g_post_mlp)
    y_prompt, new_conv_prompt, new_pool_prompt = _prompt_call(x_prompt, weights)
    return (y_prompt, y_sample, new_conv_prompt, new_pool_prompt, new_conv_sample,
            new_pool_sample)
```

```python
import jax
import jax.numpy as jnp
from jax.experimental import pallas as pl
from jax.experimental.pallas import tpu as pltpu

D_MODEL = 1024
D_FF = 4 * D_MODEL
CONV_WIDTH = 3
POOL_WINDOWS = (2, 4, 8, 16)
POOL_GROUP = D_MODEL // len(POOL_WINDOWS)
POOL_BUF = max(POOL_WINDOWS) - 1
PAST_LEN = 16384
EPS = 1e-6

CONV_HIST = 8
POOL_HIST = 16

PROMPT_TILE = 256
PROMPT_SEQS = 2
FF_CHUNK = 1024
VMEM_LIMIT_BYTES = 58 * 1024 * 1024

STREAM_ROWS = 1024
STREAM_COLS = 512
STAGE_SLOTS = 2
WBUF_SLOTS = 2


def _rmsnorm(x, g):
    r = jax.lax.rsqrt(jnp.mean(x * x, axis=-1, keepdims=True) + EPS)
    return x * r * g


def _dot(a, w):
    return jnp.dot(a.astype(jnp.bfloat16), w, preferred_element_type=jnp.float32)


def _inv_count(win, pos):
    return 1.0 / jnp.minimum(win, pos + 1).astype(jnp.float32)


def _layer_stages(w):
    (g_pre_mix, w_in, b_gate, w_conv, w_out_conv, w_pool, pool_scale, w_o,
     g_post_mix, g_pre_mlp, w_up, w_down, g_post_mlp) = w
    d = D_MODEL

    def pre_norm(st):
        st["h"] = _rmsnorm(st["x"], g_pre_mix[...]).astype(jnp.bfloat16)

    def conv_branch(st):
        h = st["h"]
        xv = _dot(h, w_in[:, 0 * d:1 * d])
        cg = _dot(h, w_in[:, 2 * d:3 * d])
        bg = _dot(h, w_in[:, 1 * d:2 * d])
        z = cg * xv
        zext = jnp.concatenate([st["z_hist"], z], axis=0)
        taps = [w_conv[:, k * d:(k + 1) * d] for k in range(CONV_WIDTH)]
        conv = zext * taps[CONV_WIDTH - 1]
        for k in range(1, CONV_WIDTH):
            conv = conv + pltpu.roll(zext, k, axis=0) * taps[CONV_WIDTH - 1 - k]
        conv = conv[CONV_HIST:]
        st["z"] = z
        st["yc"] = (bg * conv).astype(jnp.bfloat16)

    def pool_branch(st):
        h = st["h"]
        st["ya"] = _dot(st.pop("yc"), w_out_conv[...])
        u = _dot(h, w_in[:, 3 * d:4 * d])
        rows = u.shape[0]
        uext = jnp.concatenate([st["u_hist"], u], axis=0)
        pos = st["start_pos"] + jax.lax.broadcasted_iota(jnp.int32, (rows, 1), 0)
        pooled = []
        for gi, win in enumerate(POOL_WINDOWS):
            sl = slice(gi * POOL_GROUP, (gi + 1) * POOL_GROUP)
            s = uext[:, sl]
            k = 1
            while k < win:
                s = s + pltpu.roll(s, k, axis=0)
                k *= 2
            win_sum = s[POOL_HIST:]
            pooled.append((win_sum * _inv_count(win, pos) - u[:, sl]).astype(jnp.bfloat16))
        st["uext"] = uext
        st["pooled"] = pooled

    def gate_merge(st):
        h = st.pop("h")
        pooled = st.pop("pooled")
        yb = jnp.concatenate(
            [_dot(p, w_pool[gi * POOL_GROUP:(gi + 1) * POOL_GROUP, :])
             for gi, p in enumerate(pooled)], axis=-1)
        yb = yb * pool_scale[...]
        bgate = b_gate[...]
        g_conv = jax.nn.sigmoid(_dot(h, w_in[:, 4 * d:5 * d]) + bgate[:, :d])
        g_pool = jax.nn.sigmoid(_dot(h, w_in[:, 5 * d:6 * d]) + bgate[:, d:])
        st["merged"] = (g_conv * st.pop("ya") + g_pool * yb).astype(jnp.bfloat16)

    def out_proj(st):
        x1 = st["x"] + _rmsnorm(_dot(st.pop("merged"), w_o[...]), g_post_mix[...])
        st["x1"] = x1
        st["h2"] = _rmsnorm(x1, g_pre_mlp[...]).astype(jnp.bfloat16)

    def mlp_chunk(c):
        cs = slice(c * FF_CHUNK, (c + 1) * FF_CHUNK)

        def stage(st):
            a = jnp.square(jnp.maximum(_dot(st["h2"], w_up[:, cs]), 0.0))
            part = _dot(a, w_down[cs, :])
            st["f"] = part if c == 0 else st["f"] + part
        return stage

    def post_norm(st):
        st["y"] = st.pop("x1") + _rmsnorm(st.pop("f"), g_post_mlp[...])

    return ([pre_norm, conv_branch, pool_branch, gate_merge, out_proj]
            + [mlp_chunk(c) for c in range(D_FF // FF_CHUNK)] + [post_norm])


def _run_layer(tiles, w):
    stages = _layer_stages(w)
    for step in range(len(stages) + len(tiles) - 1):
        for i, st in enumerate(tiles):
            if 0 <= step - i < len(stages):
                stages[step - i](st)


def _prompt_kernel(x_ref, *refs):
    w = refs[:13]
    y_ref, conv_ref, pool_ref, z_tail, u_tail = refs[13:]
    j = pl.program_id(1)
    n_seqs, t_len, _ = x_ref.shape

    @pl.when(j == 0)
    def _():
        z_tail[...] = jnp.zeros_like(z_tail)
        u_tail[...] = jnp.zeros_like(u_tail)

    tiles = [dict(x=x_ref[s], z_hist=z_tail[s], u_hist=u_tail[s], start_pos=j * t_len)
             for s in range(n_seqs)]
    _run_layer(tiles, w)
    for s, st in enumerate(tiles):
        y_ref[s] = st["y"]
        z, uext = st["z"], st["uext"]
        z_tail[s] = z[t_len - CONV_HIST:, :]
        u_tail[s] = uext[t_len:, :]
        conv_ref[s] = z[t_len - (CONV_WIDTH - 1):, :]

    b = pl.program_id(0)
    for bb in range(pool_ref.shape[1] // n_seqs):
        @pl.when(jnp.logical_and(b == bb, j == pl.num_programs(1) - 1))
        def _(bb=bb):
            for s in range(n_seqs):
                pool_ref[:, bb * n_seqs + s, :] = u_tail[s, POOL_HIST - POOL_BUF:, :]


def _const_spec(shape):
    zeros = (0,) * len(shape)
    return pl.BlockSpec(shape, lambda b, j: zeros, pipeline_mode=pl.Buffered(1))


def _prompt_call(x, weights):
    batch, seq, d = x.shape
    tm = PROMPT_TILE
    nb = PROMPT_SEQS
    f32 = jnp.float32
    return pl.pallas_call(
        _prompt_kernel,
        grid=(batch // nb, seq // tm),
        in_specs=([pl.BlockSpec((nb, tm, d), lambda b, j: (b, j, 0))]
                  + [_const_spec(a.shape) for a in weights]),
        out_specs=[
            pl.BlockSpec((nb, tm, d), lambda b, j: (b, j, 0)),
            pl.BlockSpec((None, nb, CONV_WIDTH - 1, d), lambda b, j: (0, b, 0, 0)),
            pl.BlockSpec((POOL_BUF, batch, d), lambda b, j: (0, 0, 0)),
        ],
        out_shape=[
            jax.ShapeDtypeStruct((batch, seq, d), f32),
            jax.ShapeDtypeStruct((1, batch, CONV_WIDTH - 1, d), f32),
            jax.ShapeDtypeStruct((POOL_BUF, batch, d), f32),
        ],
        scratch_shapes=[pltpu.VMEM((nb, CONV_HIST, d), f32),
                        pltpu.VMEM((nb, POOL_HIST, d), f32)],
        compiler_params=pltpu.CompilerParams(
            dimension_semantics=("arbitrary", "arbitrary"),
            vmem_limit_bytes=VMEM_LIMIT_BYTES),
        name="prompt_layer",
    )(x, *weights)


class _Job:
    def __init__(self, copies_in, cast, copies_out, matmul, post):
        self.copies_in, self.cast, self.copies_out, self.matmul, self.post = (
            copies_in, cast, copies_out, matmul, post)


def _run_stream(jobs):
    n_jobs = len(jobs)

    def start(copies):
        for c in copies:
            c.start()

    def wait(copies):
        for c in copies:
            c.wait()

    def land(n):
        wait(jobs[n].copies_in)
        if n >= WBUF_SLOTS:
            wait(jobs[n - WBUF_SLOTS].copies_out)
        jobs[n].cast()
        start(jobs[n].copies_out)
        if n + STAGE_SLOTS < n_jobs:
            start(jobs[n + STAGE_SLOTS].copies_in)

    for n in range(min(STAGE_SLOTS, n_jobs)):
        start(jobs[n].copies_in)
    land(0)
    for n in range(n_jobs):
        if n + 1 < n_jobs:
            land(n + 1)
        if n > 0:
            jobs[n - 1].post()
        jobs[n].matmul()
    jobs[n_jobs - 1].post()
    for n in range(max(0, n_jobs - WBUF_SLOTS), n_jobs):
        wait(jobs[n].copies_out)


def _sample_kernel(x_ref, zs_ref, g_pre_mix, b_gate, w_conv, pool_scale, g_post_mix,
                   g_pre_mlp, g_post_mlp, us_hbm,
                   w_in_f, w_out_conv_f, w_pool_f, w_o_f, w_up_f, w_down_f,
                   y_ref, conv_ref, pool_hbm,
                   w_in_b, w_out_conv_b, w_pool_b, w_o_b, w_up_b, w_down_b,
                   stage, wbuf, f_ref, us_buf, u_buf, in_sems, out_sems, pool_sems):
    nseq, t_len, d = x_ref.shape
    rows = nseq * t_len
    cw = STREAM_COLS
    n_col = d // cw
    n_ff = D_FF // cw
    bf16 = jnp.bfloat16
    assert STREAM_ROWS == d and D_FF % cw == 0 and d % cw == 0 and cw % POOL_GROUP == 0
    assert CONV_WIDTH - 1 <= t_len <= POOL_BUF and nseq & (nseq - 1) == 0

    jobs = []
    st = {}

    n_carry = POOL_BUF - t_len
    carry_copy = pltpu.make_async_copy(
        us_hbm.at[pl.ds(t_len, n_carry)], pool_hbm.at[pl.ds(0, n_carry)], pool_sems.at[n_col])

    def u_copy(cb):
        return pltpu.make_async_copy(
            u_buf, pool_hbm.at[pl.ds(n_carry, t_len), :, pl.ds(cb * cw, cw)], pool_sems.at[cb])

    def us_copy(cb):
        return pltpu.make_async_copy(
            us_hbm.at[:, :, pl.ds(cb * cw, cw)], us_buf, pool_sems.at[n_col + 1 + cb])

    def gather(ref):
        return jnp.concatenate([ref[:, t, :] for t in range(t_len)], axis=0)

    def scatter(ref, val):
        for t in range(t_len):
            ref[:, t, :] = val[t * nseq:(t + 1) * nseq]

    def add_job(windows, matmul, post=None):
        n = len(jobs)
        s_slot, b_slot = n % STAGE_SLOTS, n % WBUF_SLOTS
        copies_in, copies_out, pieces = [], [], []
        for p, (src, dst, r_sl, c_sl) in enumerate(windows):
            copies_in.append(pltpu.make_async_copy(
                src, stage.at[s_slot, r_sl, c_sl], in_sems.at[s_slot, p]))
            copies_out.append(pltpu.make_async_copy(
                wbuf.at[b_slot, r_sl, c_sl], dst, out_sems.at[b_slot, p]))
            pieces.append((r_sl, c_sl))

        def cast():
            for r_sl, c_sl in pieces:
                wbuf[b_slot, r_sl, c_sl] = stage[s_slot, r_sl, c_sl].astype(bf16)

        jobs.append(_Job(copies_in, cast, copies_out, lambda: matmul(wbuf.at[b_slot]),
                         post or (lambda: None)))

    def col_job(w_f, w_b, c0, matmul, post=None):
        win = (slice(None), pl.ds(c0, cw))
        add_job([(w_f.at[win], w_b.at[win], pl.ds(0, STREAM_ROWS), pl.ds(0, cw))],
                lambda wb: matmul(wb[...]), post)

    def in_proj(name, block, cb, post=None):
        def matmul(w):
            st[name, cb] = _dot(st["h"], w)
        col_job(w_in_f, w_in_b, block * d + cb * cw, matmul, post)

    def conv_post(cb):
        cols = slice(cb * cw, (cb + 1) * cw)

        def post():
            z = st.pop(("cg", cb)) * st.pop(("xv", cb))
            z_hist = jnp.concatenate([zs_ref[:, k, cols] for k in range(CONV_WIDTH - 1)], axis=0)
            zext = jnp.concatenate([z_hist, z], axis=0)
            conv = zext[0:rows] * w_conv[:, cols]
            for k in range(1, CONV_WIDTH):
                conv = conv + zext[k * nseq:k * nseq + rows] * w_conv[:, k * d + cb * cw:
                                                                     k * d + (cb + 1) * cw]
            for k in range(CONV_WIDTH - 1):
                t = t_len - (CONV_WIDTH - 1) + k
                conv_ref[:, k, cols] = z[t * nseq:(t + 1) * nseq]
            st["yc", cb] = (st.pop(("bg", cb)) * conv).astype(bf16)
        return post

    def pool_post(cb):
        def post():
            u = st.pop(("u", cb))
            if cb > 0:
                u_copy(cb - 1).wait()
            u_buf[...] = u.reshape(t_len, nseq, cw)
            u_copy(cb).start()
            us_copy(cb).wait()
            u_hist = us_buf[...]
            uext = jnp.concatenate([u_hist.reshape(POOL_BUF * nseq, cw), u], axis=0)
            row = jax.lax.broadcasted_iota(jnp.int32, (rows, 1), 0)
            pos = PAST_LEN + (row >> (nseq.bit_length() - 1))
            for gl in range(cw // POOL_GROUP):
                gi = cb * (cw // POOL_GROUP) + gl
                win = POOL_WINDOWS[gi]
                sl = slice(gl * POOL_GROUP, (gl + 1) * POOL_GROUP)
                s = uext[:, sl]
                k, lost = 1, 0
                while k < win:
                    s = s[k * nseq:] + s[:s.shape[0] - k * nseq]
                    lost += k
                    k *= 2
                win_sum = s[(POOL_BUF - lost) * nseq:]
                st["pooled", gi] = (win_sum * _inv_count(win, pos) - u[:, sl]).astype(bf16)
            if cb + 1 < n_col:
                us_copy(cb + 1).start()
        return post

    def pool_groups():
        n_g = len(POOL_WINDOWS)

        def matmul(wb):
            yb = jnp.concatenate(
                [_dot(st.pop(("pooled", gi)),
                      wb[gi * POOL_GROUP:(gi + 1) * POOL_GROUP, 0:POOL_GROUP])
                 for gi in range(n_g)], axis=1)
            st["yb"] = yb * pool_scale[...]
        r_sl, c_sl = pl.ds(0, n_g * POOL_GROUP), pl.ds(0, POOL_GROUP)
        add_job([(w_pool_f, w_pool_b, r_sl, c_sl)], matmul)

    def out_conv_cols(cb):
        def matmul(w):
            if cb == 0:
                st["yc"] = jnp.concatenate([st.pop(("yc", c)) for c in range(n_col)], axis=1)
            yc = st["yc"] if cb + 1 < n_col else st.pop("yc")
            st["ya", cb] = _dot(yc, w)
        col_job(w_out_conv_f, w_out_conv_b, cb * cw, matmul)

    def merge_post(cb):
        cols = slice(cb * cw, (cb + 1) * cw)

        def post():
            g_conv = jax.nn.sigmoid(st.pop(("gc", cb)) + b_gate[:, cols])
            g_pool = jax.nn.sigmoid(st.pop(("gp", cb)) + b_gate[:, d + cb * cw:d + (cb + 1) * cw])
            merged = g_conv * st.pop(("ya", cb)) + g_pool * st["yb"][:, cols]
            st["merged", cb] = merged.astype(bf16)
        return post

    def out_proj_cols(cb):
        def matmul(w):
            if cb == 0:
                st.pop("h")
                st.pop("yb")
                st["merged"] = jnp.concatenate(
                    [st.pop(("merged", c)) for c in range(n_col)], axis=1)
            merged = st["merged"] if cb + 1 < n_col else st.pop("merged")
            st["mix", cb] = _dot(merged, w)

        def post():
            mix = jnp.concatenate([st.pop(("mix", c)) for c in range(n_col)], axis=1)
            x1 = gather(x_ref) + _rmsnorm(mix, g_post_mix[...])
            scatter(y_ref, x1)
            st["h2"] = _rmsnorm(x1, g_pre_mlp[...]).astype(bf16)
        col_job(w_o_f, w_o_b, cb * cw, matmul, post if cb == n_col - 1 else None)

    def mlp_up(c):
        def matmul(w):
            st["v", c] = _dot(st["h2"], w)

        def post():
            st["a", c] = jnp.square(jnp.maximum(st.pop(("v", c)), 0.0)).astype(bf16)
        col_job(w_up_f, w_up_b, c * cw, matmul, post)

    def mlp_down(c):
        windows = []
        for p in range(n_col):
            win = (pl.ds(c * cw, cw), pl.ds(p * cw, cw))
            windows.append((w_down_f.at[win], w_down_b.at[win], pl.ds(p * cw, cw), pl.ds(0, cw)))

        def matmul(wb):
            a = st.pop(("a", c))
            st["part", c] = [_dot(a, wb[p * cw:(p + 1) * cw, :]) for p in range(n_col)]

        def post():
            for p, part in enumerate(st.pop(("part", c))):
                cols = slice(p * cw, (p + 1) * cw)
                f_ref[:, cols] = part if c == 0 else f_ref[:, cols] + part
        add_job(windows, matmul, post)

    for cb in range(n_col):
        in_proj("xv", 0, cb)
        in_proj("cg", 2, cb)
        in_proj("bg", 1, cb, conv_post(cb))
    for cb in range(n_col):
        in_proj("u", 3, cb, pool_post(cb))
    in_proj("gc", 4, 0)
    pool_groups()
    for cb in range(n_col):
        out_conv_cols(cb)
        if cb > 0:
            in_proj("gc", 4, cb)
        in_proj("gp", 5, cb, merge_post(cb))
    for cb in range(n_col):
        out_proj_cols(cb)
    mlp_up(0)
    for c in range(n_ff):
        if c + 1 < n_ff:
            mlp_up(c + 1)
        mlp_down(c)

    carry_copy.start()
    us_copy(0).start()
    st["h"] = _rmsnorm(gather(x_ref), g_pre_mix[...]).astype(bf16)
    _run_stream(jobs)
    scatter(y_ref, gather(y_ref) + _rmsnorm(f_ref[...], g_post_mlp[...]))
    u_copy(n_col - 1).wait()
    carry_copy.wait()


def _sample_call(x, state_conv, state_pool_tm, vectors, w_f32):
    nseq, t_len, d = x.shape
    f32, bf16 = jnp.float32, jnp.bfloat16
    vmem = pl.BlockSpec(memory_space=pltpu.VMEM)
    hbm = pl.BlockSpec(memory_space=pl.ANY)
    n_pieces = D_MODEL // STREAM_COLS
    return pl.pallas_call(
        _sample_kernel,
        in_specs=[vmem] * (2 + len(vectors)) + [hbm] * (1 + len(w_f32)),
        out_specs=[vmem] * 2 + [hbm] * (1 + len(w_f32)),
        out_shape=[
            jax.ShapeDtypeStruct((nseq, t_len, d), f32),
            jax.ShapeDtypeStruct((nseq, CONV_WIDTH - 1, d), f32),
            jax.ShapeDtypeStruct((POOL_BUF, nseq, d), f32),
        ] + [jax.ShapeDtypeStruct(w.shape, bf16) for w in w_f32],
        scratch_shapes=[
            pltpu.VMEM((STAGE_SLOTS, STREAM_ROWS, STREAM_COLS), f32),
            pltpu.VMEM((WBUF_SLOTS, STREAM_ROWS, STREAM_COLS), bf16),
            pltpu.VMEM((nseq * t_len, d), f32),
            pltpu.VMEM((POOL_BUF, nseq, STREAM_COLS), f32),
            pltpu.VMEM((t_len, nseq, STREAM_COLS), f32),
            pltpu.SemaphoreType.DMA((STAGE_SLOTS, n_pieces)),
            pltpu.SemaphoreType.DMA((WBUF_SLOTS, n_pieces)),
            pltpu.SemaphoreType.DMA((2 * n_pieces + 1,)),
        ],
        compiler_params=pltpu.CompilerParams(vmem_limit_bytes=VMEM_LIMIT_BYTES),
        name="sample_layer",
    )(x, state_conv, *vectors, state_pool_tm, *w_f32)


def kernel(x_prompt, x_sample, state_conv, state_pool, g_pre_mix, w_in, b_gate, w_conv,
           w_out_conv, w_pool_group, pool_scale, w_o, g_post_mix, g_pre_mlp, w_up, w_down,
           g_post_mlp):
    depth = w_in.shape[0]
    assert depth == 1, "single-layer step"
    d = D_MODEL
    w_conv_flat = w_conv.reshape(1, CONV_WIDTH * d)
    vectors = (g_pre_mix, b_gate, w_conv_flat, pool_scale, g_post_mix, g_pre_mlp, g_post_mlp)
    w_f32 = (w_in[0], w_out_conv[0],
             w_pool_group[0].reshape(len(POOL_WINDOWS) * POOL_GROUP, POOL_GROUP), w_o[0],
             w_up[0], w_down[0])

    state_pool_tm = jnp.transpose(state_pool[0], (1, 0, 2))
    y_sample, conv_s, pool_tm, *w_bf16 = _sample_call(
        x_sample, state_conv[0], state_pool_tm, vectors, w_f32)
    new_conv_sample = conv_s[None]
    new_pool_sample = jnp.transpose(pool_tm, (1, 0, 2))[None]

    wb_in, wb_out_conv, wb_pool, wb_o, wb_up, wb_down = w_bf16
    weights = (g_pre_mix, wb_in, b_gate, w_conv_flat, wb_out_conv, wb_pool, pool_scale, wb_o,
               g_post_mix, g_pre_mlp, wb_up, wb_down, g_post_mlp)
    y_prompt, new_conv_prompt, pool_p_tm = _prompt_call(x_prompt, weights)
    new_pool_prompt = jnp.transpose(pool_p_tm, (1, 0, 2))[None]
    return (y_prompt, y_sample, new_conv_prompt, new_pool_prompt, new_conv_sample,
            new_pool_sample)
```

```python
import jax
import jax.numpy as jnp
from jax.experimental import pallas as pl
from jax.experimental.pallas import tpu as pltpu

D_MODEL = 1024
D_FF = 4 * D_MODEL
CONV_WIDTH = 3
POOL_WINDOWS = (2, 4, 8, 16)
POOL_GROUP = D_MODEL // len(POOL_WINDOWS)
POOL_BUF = max(POOL_WINDOWS) - 1
PAST_LEN = 16384
EPS = 1e-6

CONV_HIST = 8
POOL_HIST = 16

PROMPT_TILE = 256
PROMPT_SEQS = 2
SAMPLE_SEQS = 32
FF_CHUNK = 1024
VMEM_LIMIT_BYTES = 58 * 1024 * 1024

MATMUL_WEIGHTS = (
    ("w_in", (D_MODEL, 6 * D_MODEL)),
    ("w_out_conv", (D_MODEL, D_MODEL)),
    ("w_pool_group", (len(POOL_WINDOWS) * POOL_GROUP, POOL_GROUP)),
    ("w_o", (D_MODEL, D_MODEL)),
    ("w_up", (D_MODEL, D_FF)),
    ("w_down", (D_FF, D_MODEL)),
)
STAGE_ROWS = 256
STAGE_COLS = 1024
STAGE_SLOTS = 6


def _rmsnorm(x, g):
    r = jax.lax.rsqrt(jnp.mean(x * x, axis=-1, keepdims=True) + EPS)
    return x * r * g


def _dot(a, w):
    return jnp.dot(a.astype(jnp.bfloat16), w, preferred_element_type=jnp.float32)


def _layer_stages(w):
    (g_pre_mix, w_in, b_gate, w_conv, w_out_conv, w_pool, pool_scale, w_o,
     g_post_mix, g_pre_mlp, w_up, w_down, g_post_mlp) = w
    d = D_MODEL

    def pre_norm(st):
        st["h"] = _rmsnorm(st["x"], g_pre_mix[...]).astype(jnp.bfloat16)

    def conv_branch(st):
        h = st["h"]
        xv = _dot(h, w_in[:, 0 * d:1 * d])
        cg = _dot(h, w_in[:, 2 * d:3 * d])
        bg = _dot(h, w_in[:, 1 * d:2 * d])
        z = cg * xv
        rows = z.shape[0]
        zext = jnp.concatenate([st["z_hist"], z], axis=0)
        taps = [w_conv[:, k * d:(k + 1) * d] for k in range(CONV_WIDTH)]
        if st["time_major"]:
            ns = st["nseq"]
            conv = zext[0:rows] * taps[0]
            for k in range(1, CONV_WIDTH):
                conv = conv + zext[k * ns:k * ns + rows] * taps[k]
        else:
            conv = zext * taps[CONV_WIDTH - 1]
            for k in range(1, CONV_WIDTH):
                conv = conv + pltpu.roll(zext, k, axis=0) * taps[CONV_WIDTH - 1 - k]
            conv = conv[CONV_HIST:]
        st["emit_z"](z)
        st["yc"] = (bg * conv).astype(jnp.bfloat16)

    def pool_branch(st):
        h = st["h"]
        u = _dot(h, w_in[:, 3 * d:4 * d])
        st["ya"] = _dot(st.pop("yc"), w_out_conv[...])
        rows = u.shape[0]
        uext = jnp.concatenate([st["u_hist"], u], axis=0)
        row = jax.lax.broadcasted_iota(jnp.int32, (rows, 1), 0)
        ns = st["nseq"]
        assert ns & (ns - 1) == 0, "time-major row -> time uses a shift"
        pos = st["start_pos"] + (row >> (ns.bit_length() - 1) if st["time_major"] else row)
        pooled = []
        for gi, win in enumerate(POOL_WINDOWS):
            sl = slice(gi * POOL_GROUP, (gi + 1) * POOL_GROUP)
            s = uext[:, sl]
            k = 1
            if st["time_major"]:
                lost = 0
                while k < win:
                    n = s.shape[0]
                    s = s[k * ns:] + s[:n - k * ns]
                    lost += k
                    k *= 2
                win_sum = s[(POOL_BUF - lost) * ns:]
            else:
                while k < win:
                    s = s + pltpu.roll(s, k, axis=0)
                    k *= 2
                win_sum = s[POOL_HIST:]
            inv_count = 1.0 / jnp.minimum(win, pos + 1).astype(jnp.float32)
            pooled.append((win_sum * inv_count - u[:, sl]).astype(jnp.bfloat16))
        st["emit_uext"](uext)
        st["pooled"] = pooled

    def gate_merge(st):
        h = st.pop("h")
        pooled = st.pop("pooled")
        yb = jnp.concatenate(
            [_dot(p, w_pool[gi * POOL_GROUP:(gi + 1) * POOL_GROUP, :])
             for gi, p in enumerate(pooled)], axis=-1)
        yb = yb * pool_scale[...]
        bgate = b_gate[...]
        g_conv = jax.nn.sigmoid(_dot(h, w_in[:, 4 * d:5 * d]) + bgate[:, :d])
        g_pool = jax.nn.sigmoid(_dot(h, w_in[:, 5 * d:6 * d]) + bgate[:, d:])
        st["merged"] = (g_conv * st.pop("ya") + g_pool * yb).astype(jnp.bfloat16)

    def out_proj(st):
        x1 = st["x"] + _rmsnorm(_dot(st.pop("merged"), w_o[...]), g_post_mix[...])
        st["x1"] = x1
        st["h2"] = _rmsnorm(x1, g_pre_mlp[...]).astype(jnp.bfloat16)

    def mlp_chunk(c):
        cs = slice(c * FF_CHUNK, (c + 1) * FF_CHUNK)

        def stage(st):
            a = jnp.square(jnp.maximum(_dot(st["h2"], w_up[:, cs]), 0.0))
            part = _dot(a, w_down[cs, :])
            st["f"] = part if c == 0 else st["f"] + part
        return stage

    def post_norm(st):
        st["y"] = st.pop("x1") + _rmsnorm(st.pop("f"), g_post_mlp[...])

    return ([pre_norm, conv_branch, pool_branch, gate_merge, out_proj]
            + [mlp_chunk(c) for c in range(D_FF // FF_CHUNK)] + [post_norm])


def _run_layer(tiles, w):
    stages = _layer_stages(w)
    for step in range(len(stages) + len(tiles) - 1):
        for i, st in enumerate(tiles):
            if 0 <= step - i < len(stages):
                stages[step - i](st)


def _prompt_kernel(x_ref, *refs):
    w = refs[:13]
    y_ref, conv_ref, pool_ref, z_tail, u_tail = refs[13:]
    j = pl.program_id(1)
    n_seqs, t_len, _ = x_ref.shape

    @pl.when(j == 0)
    def _():
        z_tail[...] = jnp.zeros_like(z_tail)
        u_tail[...] = jnp.zeros_like(u_tail)

    kept = [{} for _ in range(n_seqs)]
    tiles = [dict(x=x_ref[s], z_hist=z_tail[s], u_hist=u_tail[s], time_major=False,
                  start_pos=j * t_len, nseq=1,
                  emit_z=lambda z, s=s: kept[s].update(z=z),
                  emit_uext=lambda uext, s=s: kept[s].update(uext=uext))
             for s in range(n_seqs)]
    _run_layer(tiles, w)
    for s, st in enumerate(tiles):
        y_ref[s] = st["y"]
        z, uext = kept[s]["z"], kept[s]["uext"]
        z_tail[s] = z[t_len - CONV_HIST:, :]
        u_tail[s] = uext[t_len:, :]
        conv_ref[s] = z[t_len - (CONV_WIDTH - 1):, :]

    b = pl.program_id(0)
    for bb in range(pool_ref.shape[1] // n_seqs):
        @pl.when(jnp.logical_and(b == bb, j == pl.num_programs(1) - 1))
        def _(bb=bb):
            for s in range(n_seqs):
                pool_ref[:, bb * n_seqs + s, :] = u_tail[s, POOL_HIST - POOL_BUF:, :]


def _weight_copy_jobs(w_hbm, w_bf16, stage, stage_sems, narrow_stage, narrow_sem):
    jobs = []
    n_ring = 0
    for k, (_, (n_rows, n_cols)) in enumerate(MATMUL_WEIGHTS):
        if n_cols % STAGE_COLS:
            assert narrow_stage.shape == (n_rows, n_cols)
            copy = pltpu.make_async_copy(w_hbm[k], narrow_stage, narrow_sem.at[0])
            jobs.append((copy, w_bf16[k], narrow_stage))
            continue
        for r0 in range(0, n_rows, STAGE_ROWS):
            for c0 in range(0, n_cols, STAGE_COLS):
                slot = n_ring % STAGE_SLOTS
                n_ring += 1
                window = (pl.ds(r0, STAGE_ROWS), pl.ds(c0, STAGE_COLS))
                copy = pltpu.make_async_copy(w_hbm[k].at[window], stage.at[slot],
                                             stage_sems.at[slot])
                jobs.append((copy, w_bf16[k].at[window], stage.at[slot]))
    return jobs


def _convert_weights(jobs):
    ahead = STAGE_SLOTS - 1
    for copy, _, _ in jobs[:ahead]:
        copy.start()
    for n, (copy, dst, staged) in enumerate(jobs):
        if n + ahead < len(jobs):
            jobs[n + ahead][0].start()
        copy.wait()
        dst[...] = staged[...].astype(jnp.bfloat16)


def _sample_kernel(x_ref, zs_ref, us_ref, g_pre_mix, b_gate, w_conv, pool_scale, g_post_mix,
                   g_pre_mlp, g_post_mlp, *refs):
    n_w = len(MATMUL_WEIGHTS)
    w_hbm = refs[:n_w]
    y_ref, conv_ref, pool_ref = refs[n_w:n_w + 3]
    w_out_hbm = refs[n_w + 3:2 * n_w + 3]
    w_bf16 = refs[2 * n_w + 3:3 * n_w + 3]
    stage, narrow_stage, stage_sems, narrow_sem, out_sems = refs[3 * n_w + 3:]
    i = pl.program_id(0)
    w_in, w_out_conv, w_pool, w_o, w_up, w_down = w_bf16

    def out_copy(k):
        return pltpu.make_async_copy(w_bf16[k], w_out_hbm[k], out_sems.at[k])

    @pl.when(i == 0)
    def _():
        _convert_weights(
            _weight_copy_jobs(w_hbm, w_bf16, stage, stage_sems, narrow_stage, narrow_sem))
        for k in range(n_w):
            out_copy(k).start()

    nseq, t_len, _ = x_ref.shape
    x = jnp.concatenate([x_ref[:, t, :] for t in range(t_len)], axis=0)
    z_hist = jnp.concatenate([zs_ref[:, k, :] for k in range(CONV_WIDTH - 1)], axis=0)
    u_hist = us_ref[...].reshape(POOL_BUF * nseq, D_MODEL)

    def emit_z(z):
        for k in range(CONV_WIDTH - 1):
            t = t_len - (CONV_WIDTH - 1) + k
            conv_ref[:, k, :] = z[t * nseq:(t + 1) * nseq]

    def emit_uext(uext):
        pool_ref[...] = uext[t_len * nseq:].reshape(POOL_BUF, nseq, D_MODEL)

    st = dict(x=x, z_hist=z_hist, u_hist=u_hist, time_major=True, start_pos=PAST_LEN, nseq=nseq,
              emit_z=emit_z, emit_uext=emit_uext)
    w = (g_pre_mix, w_in, b_gate, w_conv, w_out_conv, w_pool, pool_scale, w_o,
         g_post_mix, g_pre_mlp, w_up, w_down, g_post_mlp)
    _run_layer([st], w)
    y = st["y"]
    for t in range(t_len):
        y_ref[:, t, :] = y[t * nseq:(t + 1) * nseq]

    @pl.when(i == pl.num_programs(0) - 1)
    def _():
        for k in range(n_w):
            out_copy(k).wait()


def _const_spec(shape, n_grid_axes):
    zeros = (0,) * len(shape)
    if n_grid_axes == 1:
        index_map = lambda i: zeros
    else:
        index_map = lambda b, j: zeros
    return pl.BlockSpec(shape, index_map, pipeline_mode=pl.Buffered(1))


def _prompt_call(x, weights):
    batch, seq, d = x.shape
    tm = PROMPT_TILE
    nb = PROMPT_SEQS
    f32 = jnp.float32
    return pl.pallas_call(
        _prompt_kernel,
        grid=(batch // nb, seq // tm),
        in_specs=([pl.BlockSpec((nb, tm, d), lambda b, j: (b, j, 0))]
                  + [_const_spec(a.shape, 2) for a in weights]),
        out_specs=[
            pl.BlockSpec((nb, tm, d), lambda b, j: (b, j, 0)),
            pl.BlockSpec((None, nb, CONV_WIDTH - 1, d), lambda b, j: (0, b, 0, 0)),
            pl.BlockSpec((POOL_BUF, batch, d), lambda b, j: (0, 0, 0)),
        ],
        out_shape=[
            jax.ShapeDtypeStruct((batch, seq, d), f32),
            jax.ShapeDtypeStruct((1, batch, CONV_WIDTH - 1, d), f32),
            jax.ShapeDtypeStruct((POOL_BUF, batch, d), f32),
        ],
        scratch_shapes=[pltpu.VMEM((nb, CONV_HIST, d), f32),
                        pltpu.VMEM((nb, POOL_HIST, d), f32)],
        compiler_params=pltpu.CompilerParams(
            dimension_semantics=("arbitrary", "arbitrary"),
            vmem_limit_bytes=VMEM_LIMIT_BYTES),
        name="prompt_layer",
    )(x, *weights)


def _sample_call(x, state_conv, state_pool_tm, vectors, w_f32):
    nseq_total, t_len, d = x.shape
    assert CONV_WIDTH - 1 <= t_len <= POOL_BUF
    ns = SAMPLE_SEQS
    f32, bf16 = jnp.float32, jnp.bfloat16
    any_spec = pl.BlockSpec(memory_space=pl.ANY)
    w_shapes = [shape for _, shape in MATMUL_WEIGHTS]
    narrow_shapes = [shape for shape in w_shapes if shape[1] % STAGE_COLS]
    assert len(narrow_shapes) == 1
    return pl.pallas_call(
        _sample_kernel,
        grid=(nseq_total // ns,),
        in_specs=([
            pl.BlockSpec((ns, t_len, d), lambda i: (i, 0, 0)),
            pl.BlockSpec((None, ns, CONV_WIDTH - 1, d), lambda i: (0, i, 0, 0)),
            pl.BlockSpec((POOL_BUF, ns, d), lambda i: (0, i, 0)),
        ] + [_const_spec(v.shape, 1) for v in vectors] + [any_spec] * len(w_f32)),
        out_specs=[
            pl.BlockSpec((ns, t_len, d), lambda i: (i, 0, 0)),
            pl.BlockSpec((None, ns, CONV_WIDTH - 1, d), lambda i: (0, i, 0, 0)),
            pl.BlockSpec((POOL_BUF, ns, d), lambda i: (0, i, 0)),
        ] + [any_spec] * len(w_f32),
        out_shape=[
            jax.ShapeDtypeStruct((nseq_total, t_len, d), f32),
            jax.ShapeDtypeStruct((1, nseq_total, CONV_WIDTH - 1, d), f32),
            jax.ShapeDtypeStruct((POOL_BUF, nseq_total, d), f32),
        ] + [jax.ShapeDtypeStruct(s, bf16) for s in w_shapes],
        scratch_shapes=([pltpu.VMEM(s, bf16) for s in w_shapes]
                        + [pltpu.VMEM((STAGE_SLOTS, STAGE_ROWS, STAGE_COLS), f32),
                           pltpu.VMEM(narrow_shapes[0], f32),
                           pltpu.SemaphoreType.DMA((STAGE_SLOTS,)),
                           pltpu.SemaphoreType.DMA((1,)),
                           pltpu.SemaphoreType.DMA((len(w_f32),))]),
        compiler_params=pltpu.CompilerParams(
            dimension_semantics=("arbitrary",),
            vmem_limit_bytes=VMEM_LIMIT_BYTES),
        name="sample_layer",
    )(x, state_conv, state_pool_tm, *vectors, *w_f32)


def kernel(x_prompt, x_sample, state_conv, state_pool, g_pre_mix, w_in, b_gate, w_conv,
           w_out_conv, w_pool_group, pool_scale, w_o, g_post_mix, g_pre_mlp, w_up, w_down,
           g_post_mlp):
    depth = w_in.shape[0]
    assert depth == 1, "single-layer step"
    d = D_MODEL
    w_conv_flat = w_conv.reshape(1, CONV_WIDTH * d)
    vectors = (g_pre_mix, b_gate, w_conv_flat, pool_scale, g_post_mix, g_pre_mlp, g_post_mlp)
    w_f32 = (w_in[0], w_out_conv[0], w_pool_group[0].reshape(MATMUL_WEIGHTS[2][1]), w_o[0],
             w_up[0], w_down[0])

    state_pool_tm = jnp.transpose(state_pool[0], (1, 0, 2))
    y_sample, new_conv_sample, pool_tm, *w_bf16 = _sample_call(
        x_sample, state_conv, state_pool_tm, vectors, w_f32)
    new_pool_sample = jnp.transpose(pool_tm, (1, 0, 2))[None]

    wb_in, wb_out_conv, wb_pool, wb_o, wb_up, wb_down = w_bf16
    weights = (g_pre_mix, wb_in, b_gate, w_conv_flat, wb_out_conv, wb_pool, pool_scale, wb_o,
               g_post_mix, g_pre_mlp, wb_up, wb_down, g_post_mlp)
    y_prompt, new_conv_prompt, pool_tm = _prompt_call(x_prompt, weights)
    new_pool_prompt = jnp.transpose(pool_tm, (1, 0, 2))[None]
    return (y_prompt, y_sample, new_conv_prompt, new_pool_prompt, new_conv_sample,
            new_pool_sample)
```

```python
import jax
import jax.numpy as jnp
from jax.experimental import pallas as pl
from jax.experimental.pallas import tpu as pltpu

D_MODEL = 1024
D_FF = 4 * D_MODEL
CONV_WIDTH = 3
POOL_WINDOWS = (2, 4, 8, 16)
POOL_GROUP = D_MODEL // len(POOL_WINDOWS)
POOL_BUF = max(POOL_WINDOWS) - 1
PAST_LEN = 16384
EPS = 1e-6

CONV_HIST = 8
POOL_HIST = 16

PROMPT_TILE = 256
PROMPT_SEQS = 2
SAMPLE_SEQS = 32
FF_CHUNK = 1024
VMEM_LIMIT_BYTES = 58 * 1024 * 1024

MATMUL_WEIGHTS = (
    ("w_in", (D_MODEL, 6 * D_MODEL)),
    ("w_out_conv", (D_MODEL, D_MODEL)),
    ("w_pool_group", (len(POOL_WINDOWS) * POOL_GROUP, POOL_GROUP)),
    ("w_o", (D_MODEL, D_MODEL)),
    ("w_up", (D_MODEL, D_FF)),
    ("w_down", (D_FF, D_MODEL)),
)
STAGE_ROWS = 256
STAGE_COLS = 1024
STAGE_SLOTS = 6


def _rmsnorm(x, g):
    r = jax.lax.rsqrt(jnp.mean(x * x, axis=-1, keepdims=True) + EPS)
    return x * r * g


def _dot(a, w):
    return jnp.dot(a.astype(jnp.bfloat16), w, preferred_element_type=jnp.float32)


def _layer_stages(w):
    (g_pre_mix, w_in, b_gate, w_conv, w_out_conv, w_pool, pool_scale, w_o,
     g_post_mix, g_pre_mlp, w_up, w_down, g_post_mlp) = w
    d = D_MODEL

    def pre_norm(st):
        st["h"] = _rmsnorm(st["x"], g_pre_mix[...]).astype(jnp.bfloat16)

    def conv_branch(st):
        h = st["h"]
        xv = _dot(h, w_in[:, 0 * d:1 * d])
        cg = _dot(h, w_in[:, 2 * d:3 * d])
        bg = _dot(h, w_in[:, 1 * d:2 * d])
        z = cg * xv
        rows = z.shape[0]
        zext = jnp.concatenate([st["z_hist"], z], axis=0)
        taps = [w_conv[:, k * d:(k + 1) * d] for k in range(CONV_WIDTH)]
        if st["time_major"]:
            ns = st["nseq"]
            conv = zext[0:rows] * taps[0]
            for k in range(1, CONV_WIDTH):
                conv = conv + zext[k * ns:k * ns + rows] * taps[k]
        else:
            conv = zext * taps[CONV_WIDTH - 1]
            for k in range(1, CONV_WIDTH):
                conv = conv + pltpu.roll(zext, k, axis=0) * taps[CONV_WIDTH - 1 - k]
            conv = conv[CONV_HIST:]
        st["emit_z"](z)
        st["yc"] = (bg * conv).astype(jnp.bfloat16)

    def pool_branch(st):
        h = st["h"]
        u = _dot(h, w_in[:, 3 * d:4 * d])
        st["ya"] = _dot(st.pop("yc"), w_out_conv[...])
        rows = u.shape[0]
        uext = jnp.concatenate([st["u_hist"], u], axis=0)
        row = jax.lax.broadcasted_iota(jnp.int32, (rows, 1), 0)
        ns = st["nseq"]
        assert ns & (ns - 1) == 0, "time-major row -> time uses a shift"
        pos = st["start_pos"] + (row >> (ns.bit_length() - 1) if st["time_major"] else row)
        pooled = []
        for gi, win in enumerate(POOL_WINDOWS):
            sl = slice(gi * POOL_GROUP, (gi + 1) * POOL_GROUP)
            s = uext[:, sl]
            k = 1
            if st["time_major"]:
                lost = 0
                while k < win:
                    n = s.shape[0]
                    s = s[k * ns:] + s[:n - k * ns]
                    lost += k
                    k *= 2
                win_sum = s[(POOL_BUF - lost) * ns:]
            else:
                while k < win:
                    s = s + pltpu.roll(s, k, axis=0)
                    k *= 2
                win_sum = s[POOL_HIST:]
            inv_count = 1.0 / jnp.minimum(win, pos + 1).astype(jnp.float32)
            pooled.append((win_sum * inv_count - u[:, sl]).astype(jnp.bfloat16))
        st["emit_uext"](uext)
        st["pooled"] = pooled

    def gate_merge(st):
        h = st.pop("h")
        pooled = st.pop("pooled")
        bgate = b_gate[...]
        g_pool = jax.nn.sigmoid(_dot(h, w_in[:, 5 * d:6 * d]) + bgate[:, d:])
        yb = jnp.concatenate(
            [_dot(p, w_pool[gi * POOL_GROUP:(gi + 1) * POOL_GROUP, :])
             for gi, p in enumerate(pooled)], axis=-1)
        yb = yb * pool_scale[...]
        g_conv = jax.nn.sigmoid(_dot(h, w_in[:, 4 * d:5 * d]) + bgate[:, :d])
        st["merged"] = (g_conv * st.pop("ya") + g_pool * yb).astype(jnp.bfloat16)

    def out_proj(st):
        x1 = st["x"] + _rmsnorm(_dot(st.pop("merged"), w_o[...]), g_post_mix[...])
        st["x1"] = x1
        st["h2"] = _rmsnorm(x1, g_pre_mlp[...]).astype(jnp.bfloat16)

    def mlp_chunk(c):
        cs = slice(c * FF_CHUNK, (c + 1) * FF_CHUNK)

        def stage(st):
            a = jnp.square(jnp.maximum(_dot(st["h2"], w_up[:, cs]), 0.0))
            part = _dot(a, w_down[cs, :])
            st["f"] = part if c == 0 else st["f"] + part
        return stage

    def post_norm(st):
        st["y"] = st.pop("x1") + _rmsnorm(st.pop("f"), g_post_mlp[...])

    return ([pre_norm, conv_branch, pool_branch, gate_merge, out_proj]
            + [mlp_chunk(c) for c in range(D_FF // FF_CHUNK)] + [post_norm])


def _run_layer(tiles, w):
    stages = _layer_stages(w)
    for step in range(len(stages) + len(tiles) - 1):
        for i, st in enumerate(tiles):
            if 0 <= step - i < len(stages):
                stages[step - i](st)


def _prompt_kernel(x_ref, *refs):
    w = refs[:13]
    y_ref, conv_ref, pool_ref, z_tail, u_tail = refs[13:]
    j = pl.program_id(1)
    n_seqs, t_len, _ = x_ref.shape

    @pl.when(j == 0)
    def _():
        z_tail[...] = jnp.zeros_like(z_tail)
        u_tail[...] = jnp.zeros_like(u_tail)

    kept = [{} for _ in range(n_seqs)]
    tiles = [dict(x=x_ref[s], z_hist=z_tail[s], u_hist=u_tail[s], time_major=False,
                  start_pos=j * t_len, nseq=1,
                  emit_z=lambda z, s=s: kept[s].update(z=z),
                  emit_uext=lambda uext, s=s: kept[s].update(uext=uext))
             for s in range(n_seqs)]
    _run_layer(tiles, w)
    for s, st in enumerate(tiles):
        y_ref[s] = st["y"]
        z, uext = kept[s]["z"], kept[s]["uext"]
        z_tail[s] = z[t_len - CONV_HIST:, :]
        u_tail[s] = uext[t_len:, :]
        conv_ref[s] = z[t_len - (CONV_WIDTH - 1):, :]

    b = pl.program_id(0)
    for bb in range(pool_ref.shape[1] // n_seqs):
        @pl.when(jnp.logical_and(b == bb, j == pl.num_programs(1) - 1))
        def _(bb=bb):
            for s in range(n_seqs):
                pool_ref[:, bb * n_seqs + s, :] = u_tail[s, POOL_HIST - POOL_BUF:, :]


def _weight_copy_jobs(w_hbm, w_bf16, stage, stage_sems, narrow_stage, narrow_sem):
    jobs = []
    n_ring = 0
    for k, (_, (n_rows, n_cols)) in enumerate(MATMUL_WEIGHTS):
        if n_cols % STAGE_COLS:
            assert narrow_stage.shape == (n_rows, n_cols)
            copy = pltpu.make_async_copy(w_hbm[k], narrow_stage, narrow_sem.at[0])
            jobs.append((copy, w_bf16[k], narrow_stage))
            continue
        for r0 in range(0, n_rows, STAGE_ROWS):
            for c0 in range(0, n_cols, STAGE_COLS):
                slot = n_ring % STAGE_SLOTS
                n_ring += 1
                window = (pl.ds(r0, STAGE_ROWS), pl.ds(c0, STAGE_COLS))
                copy = pltpu.make_async_copy(w_hbm[k].at[window], stage.at[slot],
                                             stage_sems.at[slot])
                jobs.append((copy, w_bf16[k].at[window], stage.at[slot]))
    return jobs


def _convert_weights(jobs):
    ahead = STAGE_SLOTS - 1
    for copy, _, _ in jobs[:ahead]:
        copy.start()
    for n, (copy, dst, staged) in enumerate(jobs):
        if n + ahead < len(jobs):
            jobs[n + ahead][0].start()
        copy.wait()
        dst[...] = staged[...].astype(jnp.bfloat16)


def _sample_kernel(x_ref, zs_ref, us_ref, g_pre_mix, b_gate, w_conv, pool_scale, g_post_mix,
                   g_pre_mlp, g_post_mlp, *refs):
    n_w = len(MATMUL_WEIGHTS)
    w_hbm = refs[:n_w]
    y_ref, conv_ref, pool_ref = refs[n_w:n_w + 3]
    w_out_hbm = refs[n_w + 3:2 * n_w + 3]
    w_bf16 = refs[2 * n_w + 3:3 * n_w + 3]
    stage, narrow_stage, stage_sems, narrow_sem, out_sems = refs[3 * n_w + 3:]
    i = pl.program_id(0)
    w_in, w_out_conv, w_pool, w_o, w_up, w_down = w_bf16

    def out_copy(k):
        return pltpu.make_async_copy(w_bf16[k], w_out_hbm[k], out_sems.at[k])

    @pl.when(i == 0)
    def _():
        _convert_weights(
            _weight_copy_jobs(w_hbm, w_bf16, stage, stage_sems, narrow_stage, narrow_sem))
        for k in range(n_w):
            out_copy(k).start()

    nseq, t_len, _ = x_ref.shape
    x = jnp.concatenate([x_ref[:, t, :] for t in range(t_len)], axis=0)
    z_hist = jnp.concatenate([zs_ref[:, k, :] for k in range(CONV_WIDTH - 1)], axis=0)
    u_hist = us_ref[...].reshape(POOL_BUF * nseq, D_MODEL)

    def emit_z(z):
        for k in range(CONV_WIDTH - 1):
            t = t_len - (CONV_WIDTH - 1) + k
            conv_ref[:, k, :] = z[t * nseq:(t + 1) * nseq]

    def emit_uext(uext):
        pool_ref[...] = uext[t_len * nseq:].reshape(POOL_BUF, nseq, D_MODEL)

    st = dict(x=x, z_hist=z_hist, u_hist=u_hist, time_major=True, start_pos=PAST_LEN, nseq=nseq,
              emit_z=emit_z, emit_uext=emit_uext)
    w = (g_pre_mix, w_in, b_gate, w_conv, w_out_conv, w_pool, pool_scale, w_o,
         g_post_mix, g_pre_mlp, w_up, w_down, g_post_mlp)
    _run_layer([st], w)
    y = st["y"]
    for t in range(t_len):
        y_ref[:, t, :] = y[t * nseq:(t + 1) * nseq]

    @pl.when(i == pl.num_programs(0) - 1)
    def _():
        for k in range(n_w):
            out_copy(k).wait()


def _const_spec(shape, n_grid_axes):
    zeros = (0,) * len(shape)
    if n_grid_axes == 1:
        index_map = lambda i: zeros
    else:
        index_map = lambda b, j: zeros
    return pl.BlockSpec(shape, index_map, pipeline_mode=pl.Buffered(1))


def _prompt_call(x, weights):
    batch, seq, d = x.shape
    tm = PROMPT_TILE
    nb = PROMPT_SEQS
    f32 = jnp.float32
    return pl.pallas_call(
        _prompt_kernel,
        grid=(batch // nb, seq // tm),
        in_specs=([pl.BlockSpec((nb, tm, d), lambda b, j: (b, j, 0))]
                  + [_const_spec(a.shape, 2) for a in weights]),
        out_specs=[
            pl.BlockSpec((nb, tm, d), lambda b, j: (b, j, 0)),
            pl.BlockSpec((None, nb, CONV_WIDTH - 1, d), lambda b, j: (0, b, 0, 0)),
            pl.BlockSpec((POOL_BUF, batch, d), lambda b, j: (0, 0, 0)),
        ],
        out_shape=[
            jax.ShapeDtypeStruct((batch, seq, d), f32),
            jax.ShapeDtypeStruct((1, batch, CONV_WIDTH - 1, d), f32),
            jax.ShapeDtypeStruct((POOL_BUF, batch, d), f32),
        ],
        scratch_shapes=[pltpu.VMEM((nb, CONV_HIST, d), f32),
                        pltpu.VMEM((nb, POOL_HIST, d), f32)],
        compiler_params=pltpu.CompilerParams(
            dimension_semantics=("arbitrary", "arbitrary"),
            vmem_limit_bytes=VMEM_LIMIT_BYTES),
        name="prompt_layer",
    )(x, *weights)


def _sample_call(x, state_conv, state_pool_tm, vectors, w_f32):
    nseq_total, t_len, d = x.shape
    assert CONV_WIDTH - 1 <= t_len <= POOL_BUF
    ns = SAMPLE_SEQS
    f32, bf16 = jnp.float32, jnp.bfloat16
    any_spec = pl.BlockSpec(memory_space=pl.ANY)
    w_shapes = [shape for _, shape in MATMUL_WEIGHTS]
    narrow_shapes = [shape for shape in w_shapes if shape[1] % STAGE_COLS]
    assert len(narrow_shapes) == 1
    return pl.pallas_call(
        _sample_kernel,
        grid=(nseq_total // ns,),
        in_specs=([
            pl.BlockSpec((ns, t_len, d), lambda i: (i, 0, 0)),
            pl.BlockSpec((None, ns, CONV_WIDTH - 1, d), lambda i: (0, i, 0, 0)),
            pl.BlockSpec((POOL_BUF, ns, d), lambda i: (0, i, 0)),
        ] + [_const_spec(v.shape, 1) for v in vectors] + [any_spec] * len(w_f32)),
        out_specs=[
            pl.BlockSpec((ns, t_len, d), lambda i: (i, 0, 0)),
            pl.BlockSpec((None, ns, CONV_WIDTH - 1, d), lambda i: (0, i, 0, 0)),
            pl.BlockSpec((POOL_BUF, ns, d), lambda i: (0, i, 0)),
        ] + [any_spec] * len(w_f32),
        out_shape=[
            jax.ShapeDtypeStruct((nseq_total, t_len, d), f32),
            jax.ShapeDtypeStruct((1, nseq_total, CONV_WIDTH - 1, d), f32),
            jax.ShapeDtypeStruct((POOL_BUF, nseq_total, d), f32),
        ] + [jax.ShapeDtypeStruct(s, bf16) for s in w_shapes],
        scratch_shapes=([pltpu.VMEM(s, bf16) for s in w_shapes]
                        + [pltpu.VMEM((STAGE_SLOTS, STAGE_ROWS, STAGE_COLS), f32),
                           pltpu.VMEM(narrow_shapes[0], f32),
                           pltpu.SemaphoreType.DMA((STAGE_SLOTS,)),
                           pltpu.SemaphoreType.DMA((1,)),
                           pltpu.SemaphoreType.DMA((len(w_f32),))]),
        compiler_params=pltpu.CompilerParams(
            dimension_semantics=("arbitrary",),
            vmem_limit_bytes=VMEM_LIMIT_BYTES),
        name="sample_layer",
    )(x, state_conv, state_pool_tm, *vectors, *w_f32)


def kernel(x_prompt, x_sample, state_conv, state_pool, g_pre_mix, w_in, b_gate, w_conv,
           w_out_conv, w_pool_group, pool_scale, w_o, g_post_mix, g_pre_mlp, w_up, w_down,
           g_post_mlp):
    depth = w_in.shape[0]
    assert depth == 1, "single-layer step"
    d = D_MODEL
    w_conv_flat = w_conv.reshape(1, CONV_WIDTH * d)
    vectors = (g_pre_mix, b_gate, w_conv_flat, pool_scale, g_post_mix, g_pre_mlp, g_post_mlp)
    w_f32 = (w_in[0], w_out_conv[0], w_pool_group[0].reshape(MATMUL_WEIGHTS[2][1]), w_o[0],
             w_up[0], w_down[0])

    state_pool_tm = jnp.transpose(state_pool[0], (1, 0, 2))
    y_sample, new_conv_sample, pool_tm, *w_bf16 = _sample_call(
        x_sample, state_conv, state_pool_tm, vectors, w_f32)
    new_pool_sample = jnp.transpose(pool_tm, (1, 0, 2))[None]

    wb_in, wb_out_conv, wb_pool, wb_o, wb_up, wb_down = w_bf16
    weights = (g_pre_mix, wb_in, b_gate, w_conv_flat, wb_out_conv, wb_pool, pool_scale, wb_o,
               g_post_mix, g_pre_mlp, wb_up, wb_down, g_post_mlp)
    y_prompt, new_conv_prompt, pool_tm = _prompt_call(x_prompt, weights)
    new_pool_prompt = jnp.transpose(pool_tm, (1, 0, 2))[None]
    return (y_prompt, y_sample, new_conv_prompt, new_pool_prompt, new_conv_sample,
            new_pool_sample)
```

```python
import jax
import jax.numpy as jnp
from jax.experimental import pallas as pl
from jax.experimental.pallas import tpu as pltpu

D_MODEL = 1024
D_FF = 4 * D_MODEL
CONV_WIDTH = 3
POOL_WINDOWS = (2, 4, 8, 16)
POOL_GROUP = D_MODEL // len(POOL_WINDOWS)
POOL_BUF = max(POOL_WINDOWS) - 1
PAST_LEN = 16384
EPS = 1e-6

CONV_HIST = 8
POOL_HIST = 16

PROMPT_TILE = 256
PROMPT_SEQS = 2
FF_CHUNK = 1024
VMEM_LIMIT_BYTES = 58 * 1024 * 1024

STREAM_ROWS = 1024
STREAM_COLS = 512
STAGE_SLOTS = 4
WBUF_SLOTS = 3


def _rmsnorm(x, g):
    r = jax.lax.rsqrt(jnp.mean(x * x, axis=-1, keepdims=True) + EPS)
    return x * r * g


def _dot(a, w):
    return jnp.dot(a.astype(jnp.bfloat16), w, preferred_element_type=jnp.float32)


def _inv_count(win, pos):
    return 1.0 / jnp.minimum(win, pos + 1).astype(jnp.float32)


def _layer_stages(w):
    (g_pre_mix, w_in, b_gate, w_conv, w_out_conv, w_pool, pool_scale, w_o,
     g_post_mix, g_pre_mlp, w_up, w_down, g_post_mlp) = w
    d = D_MODEL

    def pre_norm(st):
        st["h"] = _rmsnorm(st["x"], g_pre_mix[...]).astype(jnp.bfloat16)

    def conv_branch(st):
        h = st["h"]
        xv = _dot(h, w_in[:, 0 * d:1 * d])
        cg = _dot(h, w_in[:, 2 * d:3 * d])
        bg = _dot(h, w_in[:, 1 * d:2 * d])
        z = cg * xv
        zext = jnp.concatenate([st["z_hist"], z], axis=0)
        taps = [w_conv[:, k * d:(k + 1) * d] for k in range(CONV_WIDTH)]
        conv = zext * taps[CONV_WIDTH - 1]
        for k in range(1, CONV_WIDTH):
            conv = conv + pltpu.roll(zext, k, axis=0) * taps[CONV_WIDTH - 1 - k]
        conv = conv[CONV_HIST:]
        st["z"] = z
        st["yc"] = (bg * conv).astype(jnp.bfloat16)

    def pool_branch(st):
        h = st["h"]
        u = _dot(h, w_in[:, 3 * d:4 * d])
        st["ya"] = _dot(st.pop("yc"), w_out_conv[...])
        rows = u.shape[0]
        uext = jnp.concatenate([st["u_hist"], u], axis=0)
        pos = st["start_pos"] + jax.lax.broadcasted_iota(jnp.int32, (rows, 1), 0)
        pooled = []
        for gi, win in enumerate(POOL_WINDOWS):
            sl = slice(gi * POOL_GROUP, (gi + 1) * POOL_GROUP)
            s = uext[:, sl]
            k = 1
            while k < win:
                s = s + pltpu.roll(s, k, axis=0)
                k *= 2
            win_sum = s[POOL_HIST:]
            pooled.append((win_sum * _inv_count(win, pos) - u[:, sl]).astype(jnp.bfloat16))
        st["uext"] = uext
        st["pooled"] = pooled

    def gate_merge(st):
        h = st.pop("h")
        pooled = st.pop("pooled")
        bgate = b_gate[...]
        g_pool = jax.nn.sigmoid(_dot(h, w_in[:, 5 * d:6 * d]) + bgate[:, d:])
        yb = jnp.concatenate(
            [_dot(p, w_pool[gi * POOL_GROUP:(gi + 1) * POOL_GROUP, :])
             for gi, p in enumerate(pooled)], axis=-1)
        yb = yb * pool_scale[...]
        g_conv = jax.nn.sigmoid(_dot(h, w_in[:, 4 * d:5 * d]) + bgate[:, :d])
        st["merged"] = (g_conv * st.pop("ya") + g_pool * yb).astype(jnp.bfloat16)

    def out_proj(st):
        x1 = st["x"] + _rmsnorm(_dot(st.pop("merged"), w_o[...]), g_post_mix[...])
        st["x1"] = x1
        st["h2"] = _rmsnorm(x1, g_pre_mlp[...]).astype(jnp.bfloat16)

    def mlp_chunk(c):
        cs = slice(c * FF_CHUNK, (c + 1) * FF_CHUNK)

        def stage(st):
            a = jnp.square(jnp.maximum(_dot(st["h2"], w_up[:, cs]), 0.0))
            part = _dot(a, w_down[cs, :])
            st["f"] = part if c == 0 else st["f"] + part
        return stage

    def post_norm(st):
        st["y"] = st.pop("x1") + _rmsnorm(st.pop("f"), g_post_mlp[...])

    return ([pre_norm, conv_branch, pool_branch, gate_merge, out_proj]
            + [mlp_chunk(c) for c in range(D_FF // FF_CHUNK)] + [post_norm])


def _run_layer(tiles, w):
    stages = _layer_stages(w)
    for step in range(len(stages) + len(tiles) - 1):
        for i, st in enumerate(tiles):
            if 0 <= step - i < len(stages):
                stages[step - i](st)


def _prompt_kernel(x_ref, *refs):
    w = refs[:13]
    y_ref, conv_ref, pool_ref, z_tail, u_tail = refs[13:]
    j = pl.program_id(1)
    n_seqs, t_len, _ = x_ref.shape

    @pl.when(j == 0)
    def _():
        z_tail[...] = jnp.zeros_like(z_tail)
        u_tail[...] = jnp.zeros_like(u_tail)

    tiles = [dict(x=x_ref[s], z_hist=z_tail[s], u_hist=u_tail[s], start_pos=j * t_len)
             for s in range(n_seqs)]
    _run_layer(tiles, w)
    for s, st in enumerate(tiles):
        y_ref[s] = st["y"]
        z, uext = st["z"], st["uext"]
        z_tail[s] = z[t_len - CONV_HIST:, :]
        u_tail[s] = uext[t_len:, :]
        conv_ref[s] = z[t_len - (CONV_WIDTH - 1):, :]

    b = pl.program_id(0)
    for bb in range(pool_ref.shape[1] // n_seqs):
        @pl.when(jnp.logical_and(b == bb, j == pl.num_programs(1) - 1))
        def _(bb=bb):
            for s in range(n_seqs):
                pool_ref[:, bb * n_seqs + s, :] = u_tail[s, POOL_HIST - POOL_BUF:, :]


def _const_spec(shape):
    zeros = (0,) * len(shape)
    return pl.BlockSpec(shape, lambda b, j: zeros, pipeline_mode=pl.Buffered(1))


def _prompt_call(x, weights):
    batch, seq, d = x.shape
    tm = PROMPT_TILE
    nb = PROMPT_SEQS
    f32 = jnp.float32
    return pl.pallas_call(
        _prompt_kernel,
        grid=(batch // nb, seq // tm),
        in_specs=([pl.BlockSpec((nb, tm, d), lambda b, j: (b, j, 0))]
                  + [_const_spec(a.shape) for a in weights]),
        out_specs=[
            pl.BlockSpec((nb, tm, d), lambda b, j: (b, j, 0)),
            pl.BlockSpec((None, nb, CONV_WIDTH - 1, d), lambda b, j: (0, b, 0, 0)),
            pl.BlockSpec((POOL_BUF, batch, d), lambda b, j: (0, 0, 0)),
        ],
        out_shape=[
            jax.ShapeDtypeStruct((batch, seq, d), f32),
            jax.ShapeDtypeStruct((1, batch, CONV_WIDTH - 1, d), f32),
            jax.ShapeDtypeStruct((POOL_BUF, batch, d), f32),
        ],
        scratch_shapes=[pltpu.VMEM((nb, CONV_HIST, d), f32),
                        pltpu.VMEM((nb, POOL_HIST, d), f32)],
        compiler_params=pltpu.CompilerParams(
            dimension_semantics=("arbitrary", "arbitrary"),
            vmem_limit_bytes=VMEM_LIMIT_BYTES),
        name="prompt_layer",
    )(x, *weights)


class _Job:
    def __init__(self, copies_in, cast, copies_out, matmul, post):
        self.copies_in, self.cast, self.copies_out, self.matmul, self.post = (
            copies_in, cast, copies_out, matmul, post)


def _run_stream(jobs):
    n_jobs = len(jobs)

    def start(copies):
        for c in copies:
            c.start()

    def wait(copies):
        for c in copies:
            c.wait()

    def land(n):
        wait(jobs[n].copies_in)
        if n >= WBUF_SLOTS:
            wait(jobs[n - WBUF_SLOTS].copies_out)
        jobs[n].cast()
        start(jobs[n].copies_out)
        if n + STAGE_SLOTS < n_jobs:
            start(jobs[n + STAGE_SLOTS].copies_in)

    for n in range(min(STAGE_SLOTS, n_jobs)):
        start(jobs[n].copies_in)
    land(0)
    for n in range(n_jobs):
        if n + 1 < n_jobs:
            land(n + 1)
        if n > 0:
            jobs[n - 1].post()
        jobs[n].matmul()
    jobs[n_jobs - 1].post()
    for n in range(max(0, n_jobs - WBUF_SLOTS), n_jobs):
        wait(jobs[n].copies_out)


def _sample_kernel(x_ref, zs_ref, g_pre_mix, b_gate, w_conv, pool_scale, g_post_mix,
                   g_pre_mlp, g_post_mlp, us_hbm,
                   w_in_f, w_out_conv_f, w_pool_f, w_o_f, w_up_f, w_down_f,
                   y_ref, conv_ref, pool_hbm,
                   w_in_b, w_out_conv_b, w_pool_b, w_o_b, w_up_b, w_down_b,
                   stage, wbuf, f_ref, us_buf, u_buf, in_sems, out_sems, pool_sems):
    nseq, t_len, d = x_ref.shape
    rows = nseq * t_len
    cw = STREAM_COLS
    n_col = d // cw
    n_ff = D_FF // cw
    bf16 = jnp.bfloat16
    assert STREAM_ROWS == d and D_FF % cw == 0 and d % cw == 0 and cw % POOL_GROUP == 0
    assert CONV_WIDTH - 1 <= t_len <= POOL_BUF and nseq & (nseq - 1) == 0

    jobs = []
    st = {}

    n_carry = POOL_BUF - t_len
    carry_copy = pltpu.make_async_copy(
        us_hbm.at[pl.ds(t_len, n_carry)], pool_hbm.at[pl.ds(0, n_carry)], pool_sems.at[n_col])

    def u_copy(cb):
        return pltpu.make_async_copy(
            u_buf, pool_hbm.at[pl.ds(n_carry, t_len), :, pl.ds(cb * cw, cw)], pool_sems.at[cb])

    def us_copy(cb):
        return pltpu.make_async_copy(
            us_hbm.at[:, :, pl.ds(cb * cw, cw)], us_buf, pool_sems.at[n_col + 1 + cb])

    def gather(ref):
        return jnp.concatenate([ref[:, t, :] for t in range(t_len)], axis=0)

    def scatter(ref, val):
        for t in range(t_len):
            ref[:, t, :] = val[t * nseq:(t + 1) * nseq]

    def add_job(windows, matmul, post=None):
        n = len(jobs)
        s_slot, b_slot = n % STAGE_SLOTS, n % WBUF_SLOTS
        copies_in, copies_out, pieces = [], [], []
        for p, (src, dst, r_sl, c_sl) in enumerate(windows):
            copies_in.append(pltpu.make_async_copy(
                src, stage.at[s_slot, r_sl, c_sl], in_sems.at[s_slot, p]))
            copies_out.append(pltpu.make_async_copy(
                wbuf.at[b_slot, r_sl, c_sl], dst, out_sems.at[b_slot, p]))
            pieces.append((r_sl, c_sl))

        def cast():
            for r_sl, c_sl in pieces:
                wbuf[b_slot, r_sl, c_sl] = stage[s_slot, r_sl, c_sl].astype(bf16)

        jobs.append(_Job(copies_in, cast, copies_out, lambda: matmul(wbuf.at[b_slot]),
                         post or (lambda: None)))

    def col_job(w_f, w_b, c0, matmul, post=None):
        win = (slice(None), pl.ds(c0, cw))
        add_job([(w_f.at[win], w_b.at[win], pl.ds(0, STREAM_ROWS), pl.ds(0, cw))],
                lambda wb: matmul(wb[...]), post)

    def in_proj(name, block, cb, post=None):
        def matmul(w):
            st[name, cb] = _dot(st["h"], w)
        col_job(w_in_f, w_in_b, block * d + cb * cw, matmul, post)

    def conv_post(cb):
        cols = slice(cb * cw, (cb + 1) * cw)

        def post():
            z = st.pop(("cg", cb)) * st.pop(("xv", cb))
            z_hist = jnp.concatenate([zs_ref[:, k, cols] for k in range(CONV_WIDTH - 1)], axis=0)
            zext = jnp.concatenate([z_hist, z], axis=0)
            conv = zext[0:rows] * w_conv[:, cols]
            for k in range(1, CONV_WIDTH):
                conv = conv + zext[k * nseq:k * nseq + rows] * w_conv[:, k * d + cb * cw:
                                                                     k * d + (cb + 1) * cw]
            for k in range(CONV_WIDTH - 1):
                t = t_len - (CONV_WIDTH - 1) + k
                conv_ref[:, k, cols] = z[t * nseq:(t + 1) * nseq]
            st["yc", cb] = (st.pop(("bg", cb)) * conv).astype(bf16)
        return post

    def pool_post(cb):
        def post():
            u = st.pop(("u", cb))
            if cb > 0:
                u_copy(cb - 1).wait()
            u_buf[...] = u.reshape(t_len, nseq, cw)
            u_copy(cb).start()
            us_copy(cb).wait()
            u_hist = us_buf[...]
            uext = jnp.concatenate([u_hist.reshape(POOL_BUF * nseq, cw), u], axis=0)
            row = jax.lax.broadcasted_iota(jnp.int32, (rows, 1), 0)
            pos = PAST_LEN + (row >> (nseq.bit_length() - 1))
            for gl in range(cw // POOL_GROUP):
                gi = cb * (cw // POOL_GROUP) + gl
                win = POOL_WINDOWS[gi]
                sl = slice(gl * POOL_GROUP, (gl + 1) * POOL_GROUP)
                s = uext[:, sl]
                k, lost = 1, 0
                while k < win:
                    s = s[k * nseq:] + s[:s.shape[0] - k * nseq]
                    lost += k
                    k *= 2
                win_sum = s[(POOL_BUF - lost) * nseq:]
                st["pooled", gi] = (win_sum * _inv_count(win, pos) - u[:, sl]).astype(bf16)
            if cb + 1 < n_col:
                us_copy(cb + 1).start()
        return post

    def pool_groups():
        n_g = len(POOL_WINDOWS)

        def matmul(wb):
            yb = jnp.concatenate(
                [_dot(st.pop(("pooled", gi)),
                      wb[gi * POOL_GROUP:(gi + 1) * POOL_GROUP, 0:POOL_GROUP])
                 for gi in range(n_g)], axis=1)
            st["yb"] = yb * pool_scale[...]
        r_sl, c_sl = pl.ds(0, n_g * POOL_GROUP), pl.ds(0, POOL_GROUP)
        add_job([(w_pool_f, w_pool_b, r_sl, c_sl)], matmul)

    def out_conv_cols(cb):
        def matmul(w):
            if cb == 0:
                st["yc"] = jnp.concatenate([st.pop(("yc", c)) for c in range(n_col)], axis=1)
            yc = st["yc"] if cb + 1 < n_col else st.pop("yc")
            st["ya", cb] = _dot(yc, w)
        col_job(w_out_conv_f, w_out_conv_b, cb * cw, matmul)

    def merge_post(cb):
        cols = slice(cb * cw, (cb + 1) * cw)

        def post():
            g_conv = jax.nn.sigmoid(st.pop(("gc", cb)) + b_gate[:, cols])
            g_pool = jax.nn.sigmoid(st.pop(("gp", cb)) + b_gate[:, d + cb * cw:d + (cb + 1) * cw])
            merged = g_conv * st.pop(("ya", cb)) + g_pool * st["yb"][:, cols]
            st["merged", cb] = merged.astype(bf16)
        return post

    def out_proj_cols(cb):
        def matmul(w):
            if cb == 0:
                st.pop("h")
                st.pop("yb")
                st["merged"] = jnp.concatenate(
                    [st.pop(("merged", c)) for c in range(n_col)], axis=1)
            merged = st["merged"] if cb + 1 < n_col else st.pop("merged")
            st["mix", cb] = _dot(merged, w)

        def post():
            mix = jnp.concatenate([st.pop(("mix", c)) for c in range(n_col)], axis=1)
            x1 = gather(x_ref) + _rmsnorm(mix, g_post_mix[...])
            scatter(y_ref, x1)
            st["h2"] = _rmsnorm(x1, g_pre_mlp[...]).astype(bf16)
        col_job(w_o_f, w_o_b, cb * cw, matmul, post if cb == n_col - 1 else None)

    def mlp_up(c):
        def matmul(w):
            st["v", c] = _dot(st["h2"], w)

        def post():
            st["a", c] = jnp.square(jnp.maximum(st.pop(("v", c)), 0.0)).astype(bf16)
        col_job(w_up_f, w_up_b, c * cw, matmul, post)

    def mlp_down(c):
        windows = []
        for p in range(n_col):
            win = (pl.ds(c * cw, cw), pl.ds(p * cw, cw))
            windows.append((w_down_f.at[win], w_down_b.at[win], pl.ds(p * cw, cw), pl.ds(0, cw)))

        def matmul(wb):
            a = st.pop(("a", c))
            st["part", c] = [_dot(a, wb[p * cw:(p + 1) * cw, :]) for p in range(n_col)]

        def post():
            for p, part in enumerate(st.pop(("part", c))):
                cols = slice(p * cw, (p + 1) * cw)
                f_ref[:, cols] = part if c == 0 else f_ref[:, cols] + part
        add_job(windows, matmul, post)

    for cb in range(n_col):
        in_proj("xv", 0, cb)
        in_proj("cg", 2, cb)
        in_proj("bg", 1, cb, conv_post(cb))
    for cb in range(n_col):
        in_proj("u", 3, cb, pool_post(cb))
    in_proj("gc", 4, 0)
    pool_groups()
    for cb in range(n_col):
        out_conv_cols(cb)
        if cb > 0:
            in_proj("gc", 4, cb)
        in_proj("gp", 5, cb, merge_post(cb))
    for cb in range(n_col):
        out_proj_cols(cb)
    mlp_up(0)
    for c in range(n_ff):
        if c + 1 < n_ff:
            mlp_up(c + 1)
        mlp_down(c)

    carry_copy.start()
    us_copy(0).start()
    st["h"] = _rmsnorm(gather(x_ref), g_pre_mix[...]).astype(bf16)
    _run_stream(jobs)
    scatter(y_ref, gather(y_ref) + _rmsnorm(f_ref[...], g_post_mlp[...]))
    u_copy(n_col - 1).wait()
    carry_copy.wait()


def _sample_call(x, state_conv, state_pool_tm, vectors, w_f32):
    nseq, t_len, d = x.shape
    f32, bf16 = jnp.float32, jnp.bfloat16
    vmem = pl.BlockSpec(memory_space=pltpu.VMEM)
    hbm = pl.BlockSpec(memory_space=pl.ANY)
    n_pieces = D_MODEL // STREAM_COLS
    return pl.pallas_call(
        _sample_kernel,
        in_specs=[vmem] * (2 + len(vectors)) + [hbm] * (1 + len(w_f32)),
        out_specs=[vmem] * 2 + [hbm] * (1 + len(w_f32)),
        out_shape=[
            jax.ShapeDtypeStruct((nseq, t_len, d), f32),
            jax.ShapeDtypeStruct((nseq, CONV_WIDTH - 1, d), f32),
            jax.ShapeDtypeStruct((POOL_BUF, nseq, d), f32),
        ] + [jax.ShapeDtypeStruct(w.shape, bf16) for w in w_f32],
        scratch_shapes=[
            pltpu.VMEM((STAGE_SLOTS, STREAM_ROWS, STREAM_COLS), f32),
            pltpu.VMEM((WBUF_SLOTS, STREAM_ROWS, STREAM_COLS), bf16),
            pltpu.VMEM((nseq * t_len, d), f32),
            pltpu.VMEM((POOL_BUF, nseq, STREAM_COLS), f32),
            pltpu.VMEM((t_len, nseq, STREAM_COLS), f32),
            pltpu.SemaphoreType.DMA((STAGE_SLOTS, n_pieces)),
            pltpu.SemaphoreType.DMA((WBUF_SLOTS, n_pieces)),
            pltpu.SemaphoreType.DMA((2 * n_pieces + 1,)),
        ],
        compiler_params=pltpu.CompilerParams(vmem_limit_bytes=VMEM_LIMIT_BYTES),
        name="sample_layer",
    )(x, state_conv, *vectors, state_pool_tm, *w_f32)


def kernel(x_prompt, x_sample, state_conv, state_pool, g_pre_mix, w_in, b_gate, w_conv,
           w_out_conv, w_pool_group, pool_scale, w_o, g_post_mix, g_pre_mlp, w_up, w_down,
           g_post_mlp):
    depth = w_in.shape[0]
    assert depth == 1, "single-layer step"
    d = D_MODEL
    w_conv_flat = w_conv.reshape(1, CONV_WIDTH * d)
    vectors = (g_pre_mix, b_gate, w_conv_flat, pool_scale, g_post_mix, g_pre_mlp, g_post_mlp)
    w_f32 = (w_in[0], w_out_conv[0],
             w_pool_group[0].reshape(len(POOL_WINDOWS) * POOL_GROUP, POOL_GROUP), w_o[0],
             w_up[0], w_down[0])

    state_pool_tm = jnp.transpose(state_pool[0], (1, 0, 2))
    y_sample, conv_s, pool_tm, *w_bf16 = _sample_call(
        x_sample, state_conv[0], state_pool_tm, vectors, w_f32)
    new_conv_sample = conv_s[None]
    new_pool_sample = jnp.transpose(pool_tm, (1, 0, 2))[None]

    wb_in, wb_out_conv, wb_pool, wb_o, wb_up, wb_down = w_bf16
    weights = (g_pre_mix, wb_in, b_gate, w_conv_flat, wb_out_conv, wb_pool, pool_scale, wb_o,
               g_post_mix, g_pre_mlp, wb_up, wb_down, g_post_mlp)
    y_prompt, new_conv_prompt, pool_p_tm = _prompt_call(x_prompt, weights)
    new_pool_prompt = jnp.transpose(pool_p_tm, (1, 0, 2))[None]
    return (y_prompt, y_sample, new_conv_prompt, new_pool_prompt, new_conv_sample,
            new_pool_sample)
```

```python
import jax
import jax.numpy as jnp
from jax.experimental import pallas as pl
from jax.experimental.pallas import tpu as pltpu

D_MODEL = 1024
D_FF = 4 * D_MODEL
CONV_WIDTH = 3
POOL_WINDOWS = (2, 4, 8, 16)
POOL_GROUP = D_MODEL // len(POOL_WINDOWS)
POOL_BUF = max(POOL_WINDOWS) - 1
PAST_LEN = 16384
EPS = 1e-6

CONV_HIST = 8
POOL_HIST = 16

PROMPT_TILE = 256
PROMPT_SEQS = 2
SAMPLE_SEQS = 32
FF_CHUNK = 1024
VMEM_LIMIT_BYTES = 58 * 1024 * 1024

MATMUL_WEIGHTS = (
    ("w_in", (D_MODEL, 6 * D_MODEL)),
    ("w_out_conv", (D_MODEL, D_MODEL)),
    ("w_pool_group", (len(POOL_WINDOWS) * POOL_GROUP, POOL_GROUP)),
    ("w_o", (D_MODEL, D_MODEL)),
    ("w_up", (D_MODEL, D_FF)),
    ("w_down", (D_FF, D_MODEL)),
)
STAGE_ROWS = 256
STAGE_COLS = 1024
STAGE_SLOTS = 6


def _rmsnorm(x, g):
    r = jax.lax.rsqrt(jnp.mean(x * x, axis=-1, keepdims=True) + EPS)
    return x * r * g


def _dot(a, w):
    return jnp.dot(a.astype(jnp.bfloat16), w, preferred_element_type=jnp.float32)


def _layer_stages(w):
    (g_pre_mix, w_in, b_gate, w_conv, w_out_conv, w_pool, pool_scale, w_o,
     g_post_mix, g_pre_mlp, w_up, w_down, g_post_mlp) = w
    d = D_MODEL

    def pre_norm(st):
        st["h"] = _rmsnorm(st["x"], g_pre_mix[...]).astype(jnp.bfloat16)

    def conv_branch(st):
        h = st["h"]
        xv = _dot(h, w_in[:, 0 * d:1 * d])
        cg = _dot(h, w_in[:, 2 * d:3 * d])
        bg = _dot(h, w_in[:, 1 * d:2 * d])
        z = cg * xv
        rows = z.shape[0]
        zext = jnp.concatenate([st["z_hist"], z], axis=0)
        taps = [w_conv[:, k * d:(k + 1) * d] for k in range(CONV_WIDTH)]
        if st["time_major"]:
            ns = st["nseq"]
            conv = zext[0:rows] * taps[0]
            for k in range(1, CONV_WIDTH):
                conv = conv + zext[k * ns:k * ns + rows] * taps[k]
        else:
            conv = zext * taps[CONV_WIDTH - 1]
            for k in range(1, CONV_WIDTH):
                conv = conv + pltpu.roll(zext, k, axis=0) * taps[CONV_WIDTH - 1 - k]
            conv = conv[CONV_HIST:]
        st["emit_z"](z)
        st["yc"] = (bg * conv).astype(jnp.bfloat16)

    def pool_branch(st):
        h = st["h"]
        u = _dot(h, w_in[:, 3 * d:4 * d])
        st["ya"] = _dot(st.pop("yc"), w_out_conv[...])
        rows = u.shape[0]
        uext = jnp.concatenate([st["u_hist"], u], axis=0)
        row = jax.lax.broadcasted_iota(jnp.int32, (rows, 1), 0)
        ns = st["nseq"]
        assert ns & (ns - 1) == 0, "time-major row -> time uses a shift"
        pos = st["start_pos"] + (row >> (ns.bit_length() - 1) if st["time_major"] else row)
        pooled = []
        for gi, win in enumerate(POOL_WINDOWS):
            sl = slice(gi * POOL_GROUP, (gi + 1) * POOL_GROUP)
            s = uext[:, sl]
            k = 1
            if st["time_major"]:
                lost = 0
                while k < win:
                    n = s.shape[0]
                    s = s[k * ns:] + s[:n - k * ns]
                    lost += k
                    k *= 2
                win_sum = s[(POOL_BUF - lost) * ns:]
            else:
                while k < win:
                    s = s + pltpu.roll(s, k, axis=0)
                    k *= 2
                win_sum = s[POOL_HIST:]
            inv_count = 1.0 / jnp.minimum(win, pos + 1).astype(jnp.float32)
            pooled.append((win_sum * inv_count - u[:, sl]).astype(jnp.bfloat16))
        st["emit_uext"](uext)
        st["pooled"] = pooled

    def gate_merge(st):
        h = st.pop("h")
        pooled = st.pop("pooled")
        bgate = b_gate[...]
        g_pool = jax.nn.sigmoid(_dot(h, w_in[:, 5 * d:6 * d]) + bgate[:, d:])
        yb = jnp.concatenate(
            [_dot(p, w_pool[gi * POOL_GROUP:(gi + 1) * POOL_GROUP, :])
             for gi, p in enumerate(pooled)], axis=-1)
        yb = yb * pool_scale[...]
        g_conv = jax.nn.sigmoid(_dot(h, w_in[:, 4 * d:5 * d]) + bgate[:, :d])
        st["merged"] = (g_conv * st.pop("ya") + g_pool * yb).astype(jnp.bfloat16)

    def out_proj(st):
        x1 = st["x"] + _rmsnorm(_dot(st.pop("merged"), w_o[...]), g_post_mix[...])
        st["x1"] = x1
        st["h2"] = _rmsnorm(x1, g_pre_mlp[...]).astype(jnp.bfloat16)

    def mlp_chunk(c):
        cs = slice(c * FF_CHUNK, (c + 1) * FF_CHUNK)

        def stage(st):
            a = jnp.square(jnp.maximum(_dot(st["h2"], w_up[:, cs]), 0.0))
            part = _dot(a, w_down[cs, :])
            st["f"] = part if c == 0 else st["f"] + part
        return stage

    def post_norm(st):
        st["y"] = st.pop("x1") + _rmsnorm(st.pop("f"), g_post_mlp[...])

    return ([pre_norm, conv_branch, pool_branch, gate_merge, out_proj]
            + [mlp_chunk(c) for c in range(D_FF // FF_CHUNK)] + [post_norm])


def _run_layer(tiles, w):
    stages = _layer_stages(w)
    for step in range(len(stages) + len(tiles) - 1):
        for i, st in enumerate(tiles):
            if 0 <= step - i < len(stages):
                stages[step - i](st)


def _prompt_kernel(x_hbm, *refs):
    w = refs[:13]
    y_hbm, conv_ref, pool_ref, xbuf, ybuf, z_tail, u_tail, in_sems, out_sems = refs[13:]
    _, n_seqs, t_len, _ = xbuf.shape
    n_tiles = x_hbm.shape[1] // t_len
    n_steps = (x_hbm.shape[0] // n_seqs) * n_tiles

    def hbm_tile(ref, g):
        b, j = g // n_tiles, g % n_tiles
        return ref.at[pl.ds(b * n_seqs, n_seqs), pl.ds(pl.multiple_of(j * t_len, t_len), t_len), :]

    def x_copy(g, slot):
        return pltpu.make_async_copy(hbm_tile(x_hbm, g), xbuf.at[slot], in_sems.at[slot])

    def y_copy(g, slot):
        return pltpu.make_async_copy(ybuf.at[slot], hbm_tile(y_hbm, g), out_sems.at[slot])

    def step(g, carry):
        slot = g % 2
        b, j = g // n_tiles, g % n_tiles

        @pl.when(g + 1 < n_steps)
        def _():
            x_copy(g + 1, 1 - slot).start()

        x_copy(g, slot).wait()

        @pl.when(g >= 2)
        def _():
            y_copy(g - 2, slot).wait()

        @pl.when(j == 0)
        def _():
            z_tail[...] = jnp.zeros_like(z_tail)
            u_tail[...] = jnp.zeros_like(u_tail)

        kept = [{} for _ in range(n_seqs)]
        tiles = [dict(x=xbuf[slot, s], z_hist=z_tail[s], u_hist=u_tail[s], time_major=False,
                      start_pos=j * t_len, nseq=1,
                      emit_z=lambda z, s=s: kept[s].update(z=z),
                      emit_uext=lambda uext, s=s: kept[s].update(uext=uext))
                 for s in range(n_seqs)]
        _run_layer(tiles, w)
        for s, st in enumerate(tiles):
            ybuf[slot, s] = st["y"]
            z, uext = kept[s]["z"], kept[s]["uext"]
            z_tail[s] = z[t_len - CONV_HIST:, :]
            u_tail[s] = uext[t_len:, :]
            conv_ref[0, b * n_seqs + s] = z[t_len - (CONV_WIDTH - 1):, :]
        y_copy(g, slot).start()

        for bb in range(pool_ref.shape[1] // n_seqs):
            @pl.when(jnp.logical_and(b == bb, j == n_tiles - 1))
            def _(bb=bb):
                for s in range(n_seqs):
                    pool_ref[:, bb * n_seqs + s, :] = u_tail[s, POOL_HIST - POOL_BUF:, :]
        return carry

    x_copy(0, 0).start()
    jax.lax.fori_loop(0, n_steps, step, 0)
    for g in (n_steps - 2, n_steps - 1):
        y_copy(g, g % 2).wait()


def _weight_copy_jobs(w_hbm, w_bf16, stage, stage_sems, narrow_stage, narrow_sem):
    jobs = []
    n_ring = 0
    for k, (_, (n_rows, n_cols)) in enumerate(MATMUL_WEIGHTS):
        if n_cols % STAGE_COLS:
            assert narrow_stage.shape == (n_rows, n_cols)
            copy = pltpu.make_async_copy(w_hbm[k], narrow_stage, narrow_sem.at[0])
            jobs.append((copy, w_bf16[k], narrow_stage))
            continue
        for r0 in range(0, n_rows, STAGE_ROWS):
            for c0 in range(0, n_cols, STAGE_COLS):
                slot = n_ring % STAGE_SLOTS
                n_ring += 1
                window = (pl.ds(r0, STAGE_ROWS), pl.ds(c0, STAGE_COLS))
                copy = pltpu.make_async_copy(w_hbm[k].at[window], stage.at[slot],
                                             stage_sems.at[slot])
                jobs.append((copy, w_bf16[k].at[window], stage.at[slot]))
    return jobs


def _convert_weights(jobs):
    ahead = STAGE_SLOTS - 1
    for copy, _, _ in jobs[:ahead]:
        copy.start()
    for n, (copy, dst, staged) in enumerate(jobs):
        if n + ahead < len(jobs):
            jobs[n + ahead][0].start()
        copy.wait()
        dst[...] = staged[...].astype(jnp.bfloat16)


def _sample_kernel(x_ref, zs_ref, us_ref, g_pre_mix, b_gate, w_conv, pool_scale, g_post_mix,
                   g_pre_mlp, g_post_mlp, *refs):
    n_w = len(MATMUL_WEIGHTS)
    w_hbm = refs[:n_w]
    y_ref, conv_ref, pool_ref = refs[n_w:n_w + 3]
    w_out_hbm = refs[n_w + 3:2 * n_w + 3]
    w_bf16 = refs[2 * n_w + 3:3 * n_w + 3]
    stage, narrow_stage, stage_sems, narrow_sem, out_sems = refs[3 * n_w + 3:]
    i = pl.program_id(0)
    w_in, w_out_conv, w_pool, w_o, w_up, w_down = w_bf16

    def out_copy(k):
        return pltpu.make_async_copy(w_bf16[k], w_out_hbm[k], out_sems.at[k])

    @pl.when(i == 0)
    def _():
        _convert_weights(
            _weight_copy_jobs(w_hbm, w_bf16, stage, stage_sems, narrow_stage, narrow_sem))
        for k in range(n_w):
            out_copy(k).start()

    nseq, t_len, _ = x_ref.shape
    x = jnp.concatenate([x_ref[:, t, :] for t in range(t_len)], axis=0)
    z_hist = jnp.concatenate([zs_ref[:, k, :] for k in range(CONV_WIDTH - 1)], axis=0)
    u_hist = us_ref[...].reshape(POOL_BUF * nseq, D_MODEL)

    def emit_z(z):
        for k in range(CONV_WIDTH - 1):
            t = t_len - (CONV_WIDTH - 1) + k
            conv_ref[:, k, :] = z[t * nseq:(t + 1) * nseq]

    def emit_uext(uext):
        pool_ref[...] = uext[t_len * nseq:].reshape(POOL_BUF, nseq, D_MODEL)

    st = dict(x=x, z_hist=z_hist, u_hist=u_hist, time_major=True, start_pos=PAST_LEN, nseq=nseq,
              emit_z=emit_z, emit_uext=emit_uext)
    w = (g_pre_mix, w_in, b_gate, w_conv, w_out_conv, w_pool, pool_scale, w_o,
         g_post_mix, g_pre_mlp, w_up, w_down, g_post_mlp)
    _run_layer([st], w)
    y = st["y"]
    for t in range(t_len):
        y_ref[:, t, :] = y[t * nseq:(t + 1) * nseq]

    @pl.when(i == pl.num_programs(0) - 1)
    def _():
        for k in range(n_w):
            out_copy(k).wait()


def _const_spec(shape):
    zeros = (0,) * len(shape)
    return pl.BlockSpec(shape, lambda i: zeros, pipeline_mode=pl.Buffered(1))


def _prompt_call(x, weights):
    batch, seq, d = x.shape
    tm = PROMPT_TILE
    nb = PROMPT_SEQS
    f32 = jnp.float32
    vmem = pl.BlockSpec(memory_space=pltpu.VMEM)
    hbm = pl.BlockSpec(memory_space=pl.ANY)
    return pl.pallas_call(
        _prompt_kernel,
        in_specs=[hbm] + [vmem] * len(weights),
        out_specs=[hbm, vmem, vmem],
        out_shape=[
            jax.ShapeDtypeStruct((batch, seq, d), f32),
            jax.ShapeDtypeStruct((1, batch, CONV_WIDTH - 1, d), f32),
            jax.ShapeDtypeStruct((POOL_BUF, batch, d), f32),
        ],
        scratch_shapes=[pltpu.VMEM((2, nb, tm, d), f32),
                        pltpu.VMEM((2, nb, tm, d), f32),
                        pltpu.VMEM((nb, CONV_HIST, d), f32),
                        pltpu.VMEM((nb, POOL_HIST, d), f32),
                        pltpu.SemaphoreType.DMA((2,)),
                        pltpu.SemaphoreType.DMA((2,))],
        compiler_params=pltpu.CompilerParams(vmem_limit_bytes=VMEM_LIMIT_BYTES),
        name="prompt_layer",
    )(x, *weights)


def _sample_call(x, state_conv, state_pool_tm, vectors, w_f32):
    nseq_total, t_len, d = x.shape
    assert CONV_WIDTH - 1 <= t_len <= POOL_BUF
    ns = SAMPLE_SEQS
    f32, bf16 = jnp.float32, jnp.bfloat16
    any_spec = pl.BlockSpec(memory_space=pl.ANY)
    w_shapes = [shape for _, shape in MATMUL_WEIGHTS]
    narrow_shapes = [shape for shape in w_shapes if shape[1] % STAGE_COLS]
    assert len(narrow_shapes) == 1
    return pl.pallas_call(
        _sample_kernel,
        grid=(nseq_total // ns,),
        in_specs=([
            pl.BlockSpec((ns, t_len, d), lambda i: (i, 0, 0)),
            pl.BlockSpec((None, ns, CONV_WIDTH - 1, d), lambda i: (0, i, 0, 0)),
            pl.BlockSpec((POOL_BUF, ns, d), lambda i: (0, i, 0)),
        ] + [_const_spec(v.shape) for v in vectors] + [any_spec] * len(w_f32)),
        out_specs=[
            pl.BlockSpec((ns, t_len, d), lambda i: (i, 0, 0)),
            pl.BlockSpec((None, ns, CONV_WIDTH - 1, d), lambda i: (0, i, 0, 0)),
            pl.BlockSpec((POOL_BUF, ns, d), lambda i: (0, i, 0)),
        ] + [any_spec] * len(w_f32),
        out_shape=[
            jax.ShapeDtypeStruct((nseq_total, t_len, d), f32),
            jax.ShapeDtypeStruct((1, nseq_total, CONV_WIDTH - 1, d), f32),
            jax.ShapeDtypeStruct((POOL_BUF, nseq_total, d), f32),
        ] + [jax.ShapeDtypeStruct(s, bf16) for s in w_shapes],
        scratch_shapes=([pltpu.VMEM(s, bf16) for s in w_shapes]
                        + [pltpu.VMEM((STAGE_SLOTS, STAGE_ROWS, STAGE_COLS), f32),
                           pltpu.VMEM(narrow_shapes[0], f32),
                           pltpu.SemaphoreType.DMA((STAGE_SLOTS,)),
                           pltpu.SemaphoreType.DMA((1,)),
                           pltpu.SemaphoreType.DMA((len(w_f32),))]),
        compiler_params=pltpu.CompilerParams(
            dimension_semantics=("arbitrary",),
            vmem_limit_bytes=VMEM_LIMIT_BYTES),
        name="sample_layer",
    )(x, state_conv, state_pool_tm, *vectors, *w_f32)


def kernel(x_prompt, x_sample, state_conv, state_pool, g_pre_mix, w_in, b_gate, w_conv,
           w_out_conv, w_pool_group, pool_scale, w_o, g_post_mix, g_pre_mlp, w_up, w_down,
           g_post_mlp):
    depth = w_in.shape[0]
    assert depth == 1, "single-layer step"
    d = D_MODEL
    w_conv_flat = w_conv.reshape(1, CONV_WIDTH * d)
    vectors = (g_pre_mix, b_gate, w_conv_flat, pool_scale, g_post_mix, g_pre_mlp, g_post_mlp)
    w_f32 = (w_in[0], w_out_conv[0], w_pool_group[0].reshape(MATMUL_WEIGHTS[2][1]), w_o[0],
             w_up[0], w_down[0])

    state_pool_tm = jnp.transpose(state_pool[0], (1, 0, 2))
    y_sample, new_conv_sample, pool_tm, *w_bf16 = _sample_call(
        x_sample, state_conv, state_pool_tm, vectors, w_f32)
    new_pool_sample = jnp.transpose(pool_tm, (1, 0, 2))[None]

    wb_in, wb_out_conv, wb_pool, wb_o, wb_up, wb_down = w_bf16
    weights = (g_pre_mix, wb_in, b_gate, w_conv_flat, wb_out_conv, wb_pool, pool_scale, wb_o,
               g_post_mix, g_pre_mlp, wb_up, wb_down, g_post_mlp)
    y_prompt, new_conv_prompt, pool_tm = _prompt_call(x_prompt, weights)
    new_pool_prompt = jnp.transpose(pool_tm, (1, 0, 2))[None]
    return (y_prompt, y_sample, new_conv_prompt, new_pool_prompt, new_conv_sample,
            new_pool_sample)
```

```python
import jax
import jax.numpy as jnp
from jax.experimental import pallas as pl
from jax.experimental.pallas import tpu as pltpu

D_MODEL = 1024
D_FF = 4 * D_MODEL
CONV_WIDTH = 3
POOL_WINDOWS = (2, 4, 8, 16)
POOL_GROUP = D_MODEL // len(POOL_WINDOWS)
POOL_BUF = max(POOL_WINDOWS) - 1
PAST_LEN = 16384
EPS = 1e-6

CONV_HIST = 8
POOL_HIST = 16

PROMPT_TILE = 256
PROMPT_SEQS = 2
SAMPLE_SEQS = 32
FF_CHUNK = 1024
VMEM_LIMIT_BYTES = 58 * 1024 * 1024

MATMUL_WEIGHTS = (
    ("w_in", (D_MODEL, 6 * D_MODEL)),
    ("w_out_conv", (D_MODEL, D_MODEL)),
    ("w_pool_group", (len(POOL_WINDOWS) * POOL_GROUP, POOL_GROUP)),
    ("w_o", (D_MODEL, D_MODEL)),
    ("w_up", (D_MODEL, D_FF)),
    ("w_down", (D_FF, D_MODEL)),
)
STAGE_ROWS = 256
STAGE_COLS = 1024
STAGE_SLOTS = 6


def _rmsnorm(x, g):
    r = jax.lax.rsqrt(jnp.mean(x * x, axis=-1, keepdims=True) + EPS)
    return x * r * g


def _dot(a, w):
    return jnp.dot(a.astype(jnp.bfloat16), w, preferred_element_type=jnp.float32)


def _layer_stages(w):
    (g_pre_mix, w_in, b_gate, w_conv, w_out_conv, w_pool, pool_scale, w_o,
     g_post_mix, g_pre_mlp, w_up, w_down, g_post_mlp) = w
    d = D_MODEL

    def pre_norm(st):
        st["h"] = _rmsnorm(st["x"], g_pre_mix[...]).astype(jnp.bfloat16)

    def conv_branch(st):
        h = st["h"]
        xv = _dot(h, w_in[:, 0 * d:1 * d])
        cg = _dot(h, w_in[:, 2 * d:3 * d])
        bg = _dot(h, w_in[:, 1 * d:2 * d])
        z = cg * xv
        rows = z.shape[0]
        zext = jnp.concatenate([st["z_hist"], z], axis=0)
        taps = [w_conv[:, k * d:(k + 1) * d] for k in range(CONV_WIDTH)]
        if st["time_major"]:
            ns = st["nseq"]
            conv = zext[0:rows] * taps[0]
            for k in range(1, CONV_WIDTH):
                conv = conv + zext[k * ns:k * ns + rows] * taps[k]
        else:
            conv = zext * taps[CONV_WIDTH - 1]
            for k in range(1, CONV_WIDTH):
                conv = conv + pltpu.roll(zext, k, axis=0) * taps[CONV_WIDTH - 1 - k]
            conv = conv[CONV_HIST:]
        st["emit_z"](z)
        st["yc"] = (bg * conv).astype(jnp.bfloat16)

    def pool_branch(st):
        h = st["h"]
        u = _dot(h, w_in[:, 3 * d:4 * d])
        st["ya"] = _dot(st.pop("yc"), w_out_conv[...])
        rows = u.shape[0]
        uext = jnp.concatenate([st["u_hist"], u], axis=0)
        row = jax.lax.broadcasted_iota(jnp.int32, (rows, 1), 0)
        ns = st["nseq"]
        assert ns & (ns - 1) == 0, "time-major row -> time uses a shift"
        pos = st["start_pos"] + (row >> (ns.bit_length() - 1) if st["time_major"] else row)
        pooled = []
        for gi, win in enumerate(POOL_WINDOWS):
            sl = slice(gi * POOL_GROUP, (gi + 1) * POOL_GROUP)
            s = uext[:, sl]
            k = 1
            if st["time_major"]:
                lost = 0
                while k < win:
                    n = s.shape[0]
                    s = s[k * ns:] + s[:n - k * ns]
                    lost += k
                    k *= 2
                win_sum = s[(POOL_BUF - lost) * ns:]
            else:
                while k < win:
                    s = s + pltpu.roll(s, k, axis=0)
                    k *= 2
                win_sum = s[POOL_HIST:]
            inv_count = 1.0 / jnp.minimum(win, pos + 1).astype(jnp.float32)
            pooled.append((win_sum * inv_count - u[:, sl]).astype(jnp.bfloat16))
        st["emit_uext"](uext)
        st["pooled"] = pooled

    def gate_merge(st):
        h = st.pop("h")
        pooled = st.pop("pooled")
        bgate = b_gate[...]
        g_pool = jax.nn.sigmoid(_dot(h, w_in[:, 5 * d:6 * d]) + bgate[:, d:])
        yb = jnp.concatenate(
            [_dot(p, w_pool[gi * POOL_GROUP:(gi + 1) * POOL_GROUP, :])
             for gi, p in enumerate(pooled)], axis=-1)
        yb = yb * pool_scale[...]
        g_conv = jax.nn.sigmoid(_dot(h, w_in[:, 4 * d:5 * d]) + bgate[:, :d])
        st["merged"] = (g_conv * st.pop("ya") + g_pool * yb).astype(jnp.bfloat16)

    def out_proj(st):
        x1 = st["x"] + _rmsnorm(_dot(st.pop("merged"), w_o[...]), g_post_mix[...])
        st["x1"] = x1
        st["h2"] = _rmsnorm(x1, g_pre_mlp[...]).astype(jnp.bfloat16)

    def mlp_chunk(c):
        cs = slice(c * FF_CHUNK, (c + 1) * FF_CHUNK)

        def stage(st):
            a = jnp.square(jnp.maximum(_dot(st["h2"], w_up[:, cs]), 0.0))
            part = _dot(a, w_down[cs, :])
            st["f"] = part if c == 0 else st["f"] + part
        return stage

    def post_norm(st):
        st["y"] = st.pop("x1") + _rmsnorm(st.pop("f"), g_post_mlp[...])

    return ([pre_norm, conv_branch, pool_branch, gate_merge, out_proj]
            + [mlp_chunk(c) for c in range(D_FF // FF_CHUNK)] + [post_norm])


def _run_layer(tiles, w):
    stages = _layer_stages(w)
    for step in range(len(stages) + len(tiles) - 1):
        for i, st in enumerate(tiles):
            if 0 <= step - i < len(stages):
                stages[step - i](st)


def _prompt_kernel(x_hbm, *refs):
    w = refs[:13]
    y_hbm, conv_ref, pool_ref, xbuf, ybuf, z_tail, u_tail, in_sems, out_sems = refs[13:]
    _, n_seqs, t_len, _ = xbuf.shape
    n_tiles = x_hbm.shape[1] // t_len
    n_steps = (x_hbm.shape[0] // n_seqs) * n_tiles

    def hbm_tile(ref, g):
        b, j = g // n_tiles, g % n_tiles
        return ref.at[pl.ds(b * n_seqs, n_seqs), pl.ds(pl.multiple_of(j * t_len, t_len), t_len), :]

    def x_copy(g, slot):
        return pltpu.make_async_copy(hbm_tile(x_hbm, g), xbuf.at[slot], in_sems.at[slot])

    def y_copy(g, slot):
        return pltpu.make_async_copy(ybuf.at[slot], hbm_tile(y_hbm, g), out_sems.at[slot])

    def step(g, carry):
        slot = g % 2
        b, j = g // n_tiles, g % n_tiles

        @pl.when(g + 1 < n_steps)
        def _():
            x_copy(g + 1, 1 - slot).start()

        x_copy(g, slot).wait()

        @pl.when(g >= 2)
        def _():
            y_copy(g - 2, slot).wait()

        @pl.when(j == 0)
        def _():
            z_tail[...] = jnp.zeros_like(z_tail)
            u_tail[...] = jnp.zeros_like(u_tail)

        kept = [{} for _ in range(n_seqs)]
        tiles = [dict(x=xbuf[slot, s], z_hist=z_tail[s], u_hist=u_tail[s], time_major=False,
                      start_pos=j * t_len, nseq=1,
                      emit_z=lambda z, s=s: kept[s].update(z=z),
                      emit_uext=lambda uext, s=s: kept[s].update(uext=uext))
                 for s in range(n_seqs)]
        _run_layer(tiles, w)
        for s, st in enumerate(tiles):
            ybuf[slot, s] = st["y"]
            z, uext = kept[s]["z"], kept[s]["uext"]
            z_tail[s] = z[t_len - CONV_HIST:, :]
            u_tail[s] = uext[t_len:, :]
            conv_ref[0, b * n_seqs + s] = z[t_len - (CONV_WIDTH - 1):, :]
        y_copy(g, slot).start()

        for bb in range(pool_ref.shape[1] // n_seqs):
            @pl.when(jnp.logical_and(b == bb, j == n_tiles - 1))
            def _(bb=bb):
                for s in range(n_seqs):
                    pool_ref[:, bb * n_seqs + s, :] = u_tail[s, POOL_HIST - POOL_BUF:, :]
        return carry

    x_copy(0, 0).start()
    jax.lax.fori_loop(0, n_steps, step, 0)
    for g in (n_steps - 2, n_steps - 1):
        y_copy(g, g % 2).wait()


def _weight_copy_jobs(w_hbm, w_bf16, stage, stage_sems, narrow_stage, narrow_sem):
    jobs = []
    n_ring = 0
    for k, (_, (n_rows, n_cols)) in enumerate(MATMUL_WEIGHTS):
        if n_cols % STAGE_COLS:
            assert narrow_stage.shape == (n_rows, n_cols)
            copy = pltpu.make_async_copy(w_hbm[k], narrow_stage, narrow_sem.at[0])
            jobs.append((copy, w_bf16[k], narrow_stage))
            continue
        for r0 in range(0, n_rows, STAGE_ROWS):
            for c0 in range(0, n_cols, STAGE_COLS):
                slot = n_ring % STAGE_SLOTS
                n_ring += 1
                window = (pl.ds(r0, STAGE_ROWS), pl.ds(c0, STAGE_COLS))
                copy = pltpu.make_async_copy(w_hbm[k].at[window], stage.at[slot],
                                             stage_sems.at[slot])
                jobs.append((copy, w_bf16[k].at[window], stage.at[slot]))
    return jobs


def _convert_weights(jobs):
    ahead = STAGE_SLOTS - 1
    for copy, _, _ in jobs[:ahead]:
        copy.start()
    for n, (copy, dst, staged) in enumerate(jobs):
        if n + ahead < len(jobs):
            jobs[n + ahead][0].start()
        copy.wait()
        dst[...] = staged[...].astype(jnp.bfloat16)


def _sample_kernel(x_hbm, zs_ref, us_hbm, g_pre_mix, b_gate, w_conv, pool_scale, g_post_mix,
                   g_pre_mlp, g_post_mlp, *refs):
    n_w = len(MATMUL_WEIGHTS)
    w_hbm = refs[:n_w]
    y_hbm, conv_ref, pool_hbm = refs[n_w:n_w + 3]
    w_out_hbm = refs[n_w + 3:2 * n_w + 3]
    w_bf16 = refs[2 * n_w + 3:3 * n_w + 3]
    (stage, narrow_stage, xbuf, usbuf, ybuf, poolbuf,
     stage_sems, narrow_sem, out_sems, tile_in_sems, tile_out_sems) = refs[3 * n_w + 3:]
    w_in, w_out_conv, w_pool, w_o, w_up, w_down = w_bf16
    _, nseq, t_len, _ = xbuf.shape
    n_tiles = x_hbm.shape[0] // nseq

    def weight_out_copy(k):
        return pltpu.make_async_copy(w_bf16[k], w_out_hbm[k], out_sems.at[k])

    def tile_in_copies(i, slot):
        seqs = pl.ds(pl.multiple_of(i * nseq, nseq), nseq)
        return (pltpu.make_async_copy(x_hbm.at[seqs], xbuf.at[slot], tile_in_sems.at[slot, 0]),
                pltpu.make_async_copy(us_hbm.at[:, seqs, :], usbuf.at[slot],
                                      tile_in_sems.at[slot, 1]))

    def tile_out_copies(i, slot):
        seqs = pl.ds(pl.multiple_of(i * nseq, nseq), nseq)
        return (pltpu.make_async_copy(ybuf.at[slot], y_hbm.at[seqs], tile_out_sems.at[slot, 0]),
                pltpu.make_async_copy(poolbuf.at[slot], pool_hbm.at[:, seqs, :],
                                      tile_out_sems.at[slot, 1]))

    for c in tile_in_copies(0, 0):
        c.start()
    _convert_weights(
        _weight_copy_jobs(w_hbm, w_bf16, stage, stage_sems, narrow_stage, narrow_sem))
    for k in range(n_w):
        weight_out_copy(k).start()

    w = (g_pre_mix, w_in, b_gate, w_conv, w_out_conv, w_pool, pool_scale, w_o,
         g_post_mix, g_pre_mlp, w_up, w_down, g_post_mlp)

    def tile(i, carry):
        slot = i % 2
        seqs = pl.ds(pl.multiple_of(i * nseq, nseq), nseq)

        @pl.when(i + 1 < n_tiles)
        def _():
            for c in tile_in_copies(i + 1, 1 - slot):
                c.start()

        for c in tile_in_copies(i, slot):
            c.wait()

        @pl.when(i >= 2)
        def _():
            for c in tile_out_copies(i - 2, slot):
                c.wait()

        x = jnp.concatenate([xbuf[slot, :, t, :] for t in range(t_len)], axis=0)
        z_hist = jnp.concatenate([zs_ref[0, seqs, k, :] for k in range(CONV_WIDTH - 1)], axis=0)
        u_hist = usbuf[slot].reshape(POOL_BUF * nseq, D_MODEL)

        def emit_z(z):
            for k in range(CONV_WIDTH - 1):
                t = t_len - (CONV_WIDTH - 1) + k
                conv_ref[0, seqs, k, :] = z[t * nseq:(t + 1) * nseq]

        def emit_uext(uext):
            poolbuf[slot] = uext[t_len * nseq:].reshape(POOL_BUF, nseq, D_MODEL)

        st = dict(x=x, z_hist=z_hist, u_hist=u_hist, time_major=True, start_pos=PAST_LEN,
                  nseq=nseq, emit_z=emit_z, emit_uext=emit_uext)
        _run_layer([st], w)
        y = st["y"]
        for t in range(t_len):
            ybuf[slot, :, t, :] = y[t * nseq:(t + 1) * nseq]
        for c in tile_out_copies(i, slot):
            c.start()
        return carry

    jax.lax.fori_loop(0, n_tiles, tile, 0)
    for i in range(max(0, n_tiles - 2), n_tiles):
        for c in tile_out_copies(i, i % 2):
            c.wait()
    for k in range(n_w):
        weight_out_copy(k).wait()


def _prompt_call(x, weights):
    batch, seq, d = x.shape
    tm = PROMPT_TILE
    nb = PROMPT_SEQS
    f32 = jnp.float32
    vmem = pl.BlockSpec(memory_space=pltpu.VMEM)
    hbm = pl.BlockSpec(memory_space=pl.ANY)
    return pl.pallas_call(
        _prompt_kernel,
        in_specs=[hbm] + [vmem] * len(weights),
        out_specs=[hbm, vmem, vmem],
        out_shape=[
            jax.ShapeDtypeStruct((batch, seq, d), f32),
            jax.ShapeDtypeStruct((1, batch, CONV_WIDTH - 1, d), f32),
            jax.ShapeDtypeStruct((POOL_BUF, batch, d), f32),
        ],
        scratch_shapes=[pltpu.VMEM((2, nb, tm, d), f32),
                        pltpu.VMEM((2, nb, tm, d), f32),
                        pltpu.VMEM((nb, CONV_HIST, d), f32),
                        pltpu.VMEM((nb, POOL_HIST, d), f32),
                        pltpu.SemaphoreType.DMA((2,)),
                        pltpu.SemaphoreType.DMA((2,))],
        compiler_params=pltpu.CompilerParams(vmem_limit_bytes=VMEM_LIMIT_BYTES),
        name="prompt_layer",
    )(x, *weights)


def _sample_call(x, state_conv, state_pool_tm, vectors, w_f32):
    nseq_total, t_len, d = x.shape
    assert CONV_WIDTH - 1 <= t_len <= POOL_BUF
    ns = SAMPLE_SEQS
    f32, bf16 = jnp.float32, jnp.bfloat16
    any_spec = pl.BlockSpec(memory_space=pl.ANY)
    vmem = pl.BlockSpec(memory_space=pltpu.VMEM)
    w_shapes = [shape for _, shape in MATMUL_WEIGHTS]
    narrow_shapes = [shape for shape in w_shapes if shape[1] % STAGE_COLS]
    assert len(narrow_shapes) == 1 and nseq_total % ns == 0
    return pl.pallas_call(
        _sample_kernel,
        in_specs=[any_spec, vmem, any_spec] + [vmem] * len(vectors) + [any_spec] * len(w_f32),
        out_specs=[any_spec, vmem, any_spec] + [any_spec] * len(w_f32),
        out_shape=[
            jax.ShapeDtypeStruct((nseq_total, t_len, d), f32),
            jax.ShapeDtypeStruct((1, nseq_total, CONV_WIDTH - 1, d), f32),
            jax.ShapeDtypeStruct((POOL_BUF, nseq_total, d), f32),
        ] + [jax.ShapeDtypeStruct(s, bf16) for s in w_shapes],
        scratch_shapes=([pltpu.VMEM(s, bf16) for s in w_shapes]
                        + [pltpu.VMEM((STAGE_SLOTS, STAGE_ROWS, STAGE_COLS), f32),
                           pltpu.VMEM(narrow_shapes[0], f32),
                           pltpu.VMEM((2, ns, t_len, d), f32),
                           pltpu.VMEM((2, POOL_BUF, ns, d), f32),
                           pltpu.VMEM((2, ns, t_len, d), f32),
                           pltpu.VMEM((2, POOL_BUF, ns, d), f32),
                           pltpu.SemaphoreType.DMA((STAGE_SLOTS,)),
                           pltpu.SemaphoreType.DMA((1,)),
                           pltpu.SemaphoreType.DMA((len(w_f32),)),
                           pltpu.SemaphoreType.DMA((2, 2)),
                           pltpu.SemaphoreType.DMA((2, 2))]),
        compiler_params=pltpu.CompilerParams(vmem_limit_bytes=VMEM_LIMIT_BYTES),
        name="sample_layer",
    )(x, state_conv, state_pool_tm, *vectors, *w_f32)


def kernel(x_prompt, x_sample, state_conv, state_pool, g_pre_mix, w_in, b_gate, w_conv,
           w_out_conv, w_pool_group, pool_scale, w_o, g_post_mix, g_pre_mlp, w_up, w_down,
           g_post_mlp):
    depth = w_in.shape[0]
    assert depth == 1, "single-layer step"
    d = D_MODEL
    w_conv_flat = w_conv.reshape(1, CONV_WIDTH * d)
    vectors = (g_pre_mix, b_gate, w_conv_flat, pool_scale, g_post_mix, g_pre_mlp, g_post_mlp)
    w_f32 = (w_in[0], w_out_conv[0], w_pool_group[0].reshape(MATMUL_WEIGHTS[2][1]), w_o[0],
             w_up[0], w_down[0])

    state_pool_tm = jnp.transpose(state_pool[0], (1, 0, 2))
    y_sample, new_conv_sample, pool_tm, *w_bf16 = _sample_call(
        x_sample, state_conv, state_pool_tm, vectors, w_f32)
    new_pool_sample = jnp.transpose(pool_tm, (1, 0, 2))[None]

    wb_in, wb_out_conv, wb_pool, wb_o, wb_up, wb_down = w_bf16
    weights = (g_pre_mix, wb_in, b_gate, w_conv_flat, wb_out_conv, wb_pool, pool_scale, wb_o,
               g_post_mix, g_pre_mlp, wb_up, wb_down, g_post_mlp)
    y_prompt, new_conv_prompt, pool_tm = _prompt_call(x_prompt, weights)
    new_pool_prompt = jnp.transpose(pool_tm, (1, 0, 2))[None]
    return (y_prompt, y_sample, new_conv_prompt, new_pool_prompt, new_conv_sample,
            new_pool_sample)
```

```python
import jax
import jax.numpy as jnp
from jax.experimental import pallas as pl
from jax.experimental.pallas import tpu as pltpu

D_MODEL = 1024
D_FF = 4 * D_MODEL
CONV_WIDTH = 3
POOL_WINDOWS = (2, 4, 8, 16)
POOL_GROUP = D_MODEL // len(POOL_WINDOWS)
POOL_BUF = max(POOL_WINDOWS) - 1
PAST_LEN = 16384
EPS = 1e-6

CONV_HIST = 8
POOL_HIST = 16

PROMPT_TILE = 256
PROMPT_SEQS = 2
SAMPLE_SEQS = 32
FF_CHUNK = 1024
VMEM_LIMIT_BYTES = 58 * 1024 * 1024

MATMUL_WEIGHTS = (
    ("w_in", (D_MODEL, 6 * D_MODEL)),
    ("w_out_conv", (D_MODEL, D_MODEL)),
    ("w_pool_group", (len(POOL_WINDOWS) * POOL_GROUP, POOL_GROUP)),
    ("w_o", (D_MODEL, D_MODEL)),
    ("w_up", (D_MODEL, D_FF)),
    ("w_down", (D_FF, D_MODEL)),
)
STAGE_ROWS = 256
STAGE_COLS = 1024
STAGE_SLOTS = 6


def _rmsnorm(x, g):
    r = jax.lax.rsqrt(jnp.mean(x * x, axis=-1, keepdims=True) + EPS)
    return x * r * g


def _dot(a, w):
    return jnp.dot(a.astype(jnp.bfloat16), w, preferred_element_type=jnp.float32)


def _layer_stages(w):
    (g_pre_mix, w_in, b_gate, w_conv, w_out_conv, w_pool, pool_scale, w_o,
     g_post_mix, g_pre_mlp, w_up, w_down, g_post_mlp) = w
    d = D_MODEL

    def pre_norm(st):
        st["h"] = _rmsnorm(st["x"], g_pre_mix[...]).astype(jnp.bfloat16)

    def conv_branch(st):
        h = st["h"]
        xv = _dot(h, w_in[:, 0 * d:1 * d])
        cg = _dot(h, w_in[:, 2 * d:3 * d])
        bg = _dot(h, w_in[:, 1 * d:2 * d])
        z = cg * xv
        rows = z.shape[0]
        zext = jnp.concatenate([st["z_hist"], z], axis=0)
        taps = [w_conv[:, k * d:(k + 1) * d] for k in range(CONV_WIDTH)]
        if st["time_major"]:
            ns = st["nseq"]
            conv = zext[0:rows] * taps[0]
            for k in range(1, CONV_WIDTH):
                conv = conv + zext[k * ns:k * ns + rows] * taps[k]
        else:
            conv = zext * taps[CONV_WIDTH - 1]
            for k in range(1, CONV_WIDTH):
                conv = conv + pltpu.roll(zext, k, axis=0) * taps[CONV_WIDTH - 1 - k]
            conv = conv[CONV_HIST:]
        st["emit_z"](z)
        st["yc"] = (bg * conv).astype(jnp.bfloat16)

    def pool_branch(st):
        h = st["h"]
        u = _dot(h, w_in[:, 3 * d:4 * d])
        st["ya"] = _dot(st.pop("yc"), w_out_conv[...])
        rows = u.shape[0]
        uext = jnp.concatenate([st["u_hist"], u], axis=0)
        row = jax.lax.broadcasted_iota(jnp.int32, (rows, 1), 0)
        ns = st["nseq"]
        assert ns & (ns - 1) == 0, "time-major row -> time uses a shift"
        pos = st["start_pos"] + (row >> (ns.bit_length() - 1) if st["time_major"] else row)
        pooled = []
        for gi, win in enumerate(POOL_WINDOWS):
            sl = slice(gi * POOL_GROUP, (gi + 1) * POOL_GROUP)
            s = uext[:, sl]
            k = 1
            if st["time_major"]:
                lost = 0
                while k < win:
                    n = s.shape[0]
                    s = s[k * ns:] + s[:n - k * ns]
                    lost += k
                    k *= 2
                win_sum = s[(POOL_BUF - lost) * ns:]
            else:
                while k < win:
                    s = s + pltpu.roll(s, k, axis=0)
                    k *= 2
                win_sum = s[POOL_HIST:]
            inv_count = 1.0 / jnp.minimum(win, pos + 1).astype(jnp.float32)
            pooled.append((win_sum * inv_count - u[:, sl]).astype(jnp.bfloat16))
        st["emit_uext"](uext)
        st["pooled"] = pooled

    def gate_merge(st):
        h = st.pop("h")
        pooled = st.pop("pooled")
        bgate = b_gate[...]
        g_pool = jax.nn.sigmoid(_dot(h, w_in[:, 5 * d:6 * d]) + bgate[:, d:])
        yb = jnp.concatenate(
            [_dot(p, w_pool[gi * POOL_GROUP:(gi + 1) * POOL_GROUP, :])
             for gi, p in enumerate(pooled)], axis=-1)
        yb = yb * pool_scale[...]
        g_conv = jax.nn.sigmoid(_dot(h, w_in[:, 4 * d:5 * d]) + bgate[:, :d])
        st["merged"] = (g_conv * st.pop("ya") + g_pool * yb).astype(jnp.bfloat16)

    def out_proj(st):
        x1 = st["x"] + _rmsnorm(_dot(st.pop("merged"), w_o[...]), g_post_mix[...])
        st["x1"] = x1
        st["h2"] = _rmsnorm(x1, g_pre_mlp[...]).astype(jnp.bfloat16)

    def mlp_chunk(c):
        cs = slice(c * FF_CHUNK, (c + 1) * FF_CHUNK)

        def stage(st):
            a = jnp.square(jnp.maximum(_dot(st["h2"], w_up[:, cs]), 0.0))
            part = _dot(a, w_down[cs, :])
            st["f"] = part if c == 0 else st["f"] + part
        return stage

    def post_norm(st):
        st["y"] = st.pop("x1") + _rmsnorm(st.pop("f"), g_post_mlp[...])

    return ([pre_norm, conv_branch, pool_branch, gate_merge, out_proj]
            + [mlp_chunk(c) for c in range(D_FF // FF_CHUNK)] + [post_norm])


def _run_layer(tiles, w):
    stages = _layer_stages(w)
    for step in range(len(stages) + len(tiles) - 1):
        for i, st in enumerate(tiles):
            if 0 <= step - i < len(stages):
                stages[step - i](st)


def _prompt_kernel(x_hbm, *refs):
    w = refs[:13]
    y_hbm, conv_ref, pool_ref, xbuf, ybuf, z_tail, u_tail, in_sems, out_sems = refs[13:]
    _, n_seqs, t_len, _ = xbuf.shape
    n_tiles = x_hbm.shape[1] // t_len
    n_steps = (x_hbm.shape[0] // n_seqs) * n_tiles

    def hbm_tile(ref, g):
        b, j = g // n_tiles, g % n_tiles
        return ref.at[pl.ds(b * n_seqs, n_seqs), pl.ds(pl.multiple_of(j * t_len, t_len), t_len), :]

    def x_copy(g, slot):
        return pltpu.make_async_copy(hbm_tile(x_hbm, g), xbuf.at[slot], in_sems.at[slot])

    def y_copy(g, slot):
        return pltpu.make_async_copy(ybuf.at[slot], hbm_tile(y_hbm, g), out_sems.at[slot])

    def step(g, carry):
        slot = g % 2
        b, j = g // n_tiles, g % n_tiles

        @pl.when(g + 1 < n_steps)
        def _():
            x_copy(g + 1, 1 - slot).start()

        x_copy(g, slot).wait()

        @pl.when(g >= 2)
        def _():
            y_copy(g - 2, slot).wait()

        @pl.when(j == 0)
        def _():
            z_tail[...] = jnp.zeros_like(z_tail)
            u_tail[...] = jnp.zeros_like(u_tail)

        kept = [{} for _ in range(n_seqs)]
        tiles = [dict(x=xbuf[slot, s], z_hist=z_tail[s], u_hist=u_tail[s], time_major=False,
                      start_pos=j * t_len, nseq=1,
                      emit_z=lambda z, s=s: kept[s].update(z=z),
                      emit_uext=lambda uext, s=s: kept[s].update(uext=uext))
                 for s in range(n_seqs)]
        _run_layer(tiles, w)
        for s, st in enumerate(tiles):
            ybuf[slot, s] = st["y"]
            z, uext = kept[s]["z"], kept[s]["uext"]
            z_tail[s] = z[t_len - CONV_HIST:, :]
            u_tail[s] = uext[t_len:, :]
            conv_ref[0, b * n_seqs + s] = z[t_len - (CONV_WIDTH - 1):, :]
        y_copy(g, slot).start()

        @pl.when(j == n_tiles - 1)
        def _():
            for s in range(n_seqs):
                rows = u_tail[s, POOL_HIST - POOL_BUF:, :]
                pool_ref[:, pl.ds(b * n_seqs + s, 1), :] = rows.reshape(POOL_BUF, 1, rows.shape[-1])
        return carry

    x_copy(0, 0).start()
    jax.lax.fori_loop(0, n_steps, step, 0)
    for g in (n_steps - 2, n_steps - 1):
        y_copy(g, g % 2).wait()


def _weight_copy_jobs(w_hbm, w_bf16, stage, stage_sems, narrow_stage, narrow_sem):
    jobs = []
    n_ring = 0
    for k, (_, (n_rows, n_cols)) in enumerate(MATMUL_WEIGHTS):
        if n_cols % STAGE_COLS:
            assert narrow_stage.shape == (n_rows, n_cols)
            copy = pltpu.make_async_copy(w_hbm[k], narrow_stage, narrow_sem.at[0])
            jobs.append((copy, w_bf16[k], narrow_stage))
            continue
        for r0 in range(0, n_rows, STAGE_ROWS):
            for c0 in range(0, n_cols, STAGE_COLS):
                slot = n_ring % STAGE_SLOTS
                n_ring += 1
                window = (pl.ds(r0, STAGE_ROWS), pl.ds(c0, STAGE_COLS))
                copy = pltpu.make_async_copy(w_hbm[k].at[window], stage.at[slot],
                                             stage_sems.at[slot])
                jobs.append((copy, w_bf16[k].at[window], stage.at[slot]))
    return jobs


def _convert_weights(jobs):
    ahead = STAGE_SLOTS - 1
    for copy, _, _ in jobs[:ahead]:
        copy.start()
    for n, (copy, dst, staged) in enumerate(jobs):
        if n + ahead < len(jobs):
            jobs[n + ahead][0].start()
        copy.wait()
        dst[...] = staged[...].astype(jnp.bfloat16)


def _sample_kernel(x_ref, zs_ref, us_ref, g_pre_mix, b_gate, w_conv, pool_scale, g_post_mix,
                   g_pre_mlp, g_post_mlp, *refs):
    n_w = len(MATMUL_WEIGHTS)
    w_hbm = refs[:n_w]
    y_ref, conv_ref, pool_ref = refs[n_w:n_w + 3]
    w_out_hbm = refs[n_w + 3:2 * n_w + 3]
    w_bf16 = refs[2 * n_w + 3:3 * n_w + 3]
    stage, narrow_stage, stage_sems, narrow_sem, out_sems = refs[3 * n_w + 3:]
    i = pl.program_id(0)
    w_in, w_out_conv, w_pool, w_o, w_up, w_down = w_bf16

    def out_copy(k):
        return pltpu.make_async_copy(w_bf16[k], w_out_hbm[k], out_sems.at[k])

    @pl.when(i == 0)
    def _():
        _convert_weights(
            _weight_copy_jobs(w_hbm, w_bf16, stage, stage_sems, narrow_stage, narrow_sem))
        for k in range(n_w):
            out_copy(k).start()

    nseq, t_len, _ = x_ref.shape
    x = jnp.concatenate([x_ref[:, t, :] for t in range(t_len)], axis=0)
    z_hist = jnp.concatenate([zs_ref[:, k, :] for k in range(CONV_WIDTH - 1)], axis=0)
    u_hist = us_ref[...].reshape(POOL_BUF * nseq, D_MODEL)

    def emit_z(z):
        for k in range(CONV_WIDTH - 1):
            t = t_len - (CONV_WIDTH - 1) + k
            conv_ref[:, k, :] = z[t * nseq:(t + 1) * nseq]

    def emit_uext(uext):
        pool_ref[...] = uext[t_len * nseq:].reshape(POOL_BUF, nseq, D_MODEL)

    st = dict(x=x, z_hist=z_hist, u_hist=u_hist, time_major=True, start_pos=PAST_LEN, nseq=nseq,
              emit_z=emit_z, emit_uext=emit_uext)
    w = (g_pre_mix, w_in, b_gate, w_conv, w_out_conv, w_pool, pool_scale, w_o,
         g_post_mix, g_pre_mlp, w_up, w_down, g_post_mlp)
    _run_layer([st], w)
    y = st["y"]
    for t in range(t_len):
        y_ref[:, t, :] = y[t * nseq:(t + 1) * nseq]

    @pl.when(i == pl.num_programs(0) - 1)
    def _():
        for k in range(n_w):
            out_copy(k).wait()


def _const_spec(shape):
    zeros = (0,) * len(shape)
    return pl.BlockSpec(shape, lambda i: zeros, pipeline_mode=pl.Buffered(1))


def _prompt_call(x, weights):
    batch, seq, d = x.shape
    tm = PROMPT_TILE
    nb = PROMPT_SEQS
    f32 = jnp.float32
    vmem = pl.BlockSpec(memory_space=pltpu.VMEM)
    hbm = pl.BlockSpec(memory_space=pl.ANY)
    return pl.pallas_call(
        _prompt_kernel,
        in_specs=[hbm] + [vmem] * len(weights),
        out_specs=[hbm, vmem, vmem],
        out_shape=[
            jax.ShapeDtypeStruct((batch, seq, d), f32),
            jax.ShapeDtypeStruct((1, batch, CONV_WIDTH - 1, d), f32),
            jax.ShapeDtypeStruct((POOL_BUF, batch, d), f32),
        ],
        scratch_shapes=[pltpu.VMEM((2, nb, tm, d), f32),
                        pltpu.VMEM((2, nb, tm, d), f32),
                        pltpu.VMEM((nb, CONV_HIST, d), f32),
                        pltpu.VMEM((nb, POOL_HIST, d), f32),
                        pltpu.SemaphoreType.DMA((2,)),
                        pltpu.SemaphoreType.DMA((2,))],
        compiler_params=pltpu.CompilerParams(vmem_limit_bytes=VMEM_LIMIT_BYTES),
        name="prompt_layer",
    )(x, *weights)


def _sample_call(x, state_conv, state_pool_tm, vectors, w_f32):
    nseq_total, t_len, d = x.shape
    assert CONV_WIDTH - 1 <= t_len <= POOL_BUF
    ns = SAMPLE_SEQS
    f32, bf16 = jnp.float32, jnp.bfloat16
    any_spec = pl.BlockSpec(memory_space=pl.ANY)
    w_shapes = [shape for _, shape in MATMUL_WEIGHTS]
    narrow_shapes = [shape for shape in w_shapes if shape[1] % STAGE_COLS]
    assert len(narrow_shapes) == 1
    return pl.pallas_call(
        _sample_kernel,
        grid=(nseq_total // ns,),
        in_specs=([
            pl.BlockSpec((ns, t_len, d), lambda i: (i, 0, 0)),
            pl.BlockSpec((None, ns, CONV_WIDTH - 1, d), lambda i: (0, i, 0, 0)),
            pl.BlockSpec((POOL_BUF, ns, d), lambda i: (0, i, 0)),
        ] + [_const_spec(v.shape) for v in vectors] + [any_spec] * len(w_f32)),
        out_specs=[
            pl.BlockSpec((ns, t_len, d), lambda i: (i, 0, 0)),
            pl.BlockSpec((None, ns, CONV_WIDTH - 1, d), lambda i: (0, i, 0, 0)),
            pl.BlockSpec((POOL_BUF, ns, d), lambda i: (0, i, 0)),
        ] + [any_spec] * len(w_f32),
        out_shape=[
            jax.ShapeDtypeStruct((nseq_total, t_len, d), f32),
            jax.ShapeDtypeStruct((1, nseq_total, CONV_WIDTH - 1, d), f32),
            jax.ShapeDtypeStruct((POOL_BUF, nseq_total, d), f32),
        ] + [jax.ShapeDtypeStruct(s, bf16) for s in w_shapes],
        scratch_shapes=([pltpu.VMEM(s, bf16) for s in w_shapes]
                        + [pltpu.VMEM((STAGE_SLOTS, STAGE_ROWS, STAGE_COLS), f32),
                           pltpu.VMEM(narrow_shapes[0], f32),
                           pltpu.SemaphoreType.DMA((STAGE_SLOTS,)),
                           pltpu.SemaphoreType.DMA((1,)),
                           pltpu.SemaphoreType.DMA((len(w_f32),))]),
        compiler_params=pltpu.CompilerParams(
            dimension_semantics=("arbitrary",),
            vmem_limit_bytes=VMEM_LIMIT_BYTES),
        name="sample_layer",
    )(x, state_conv, state_pool_tm, *vectors, *w_f32)


def kernel(x_prompt, x_sample, state_conv, state_pool, g_pre_mix, w_in, b_gate, w_conv,
           w_out_conv, w_pool_group, pool_scale, w_o, g_post_mix, g_pre_mlp, w_up, w_down,
           g_post_mlp):
    depth = w_in.shape[0]
    assert depth == 1, "single-layer step"
    d = D_MODEL
    w_conv_flat = w_conv.reshape(1, CONV_WIDTH * d)
    vectors = (g_pre_mix, b_gate, w_conv_flat, pool_scale, g_post_mix, g_pre_mlp, g_post_mlp)
    w_f32 = (w_in[0], w_out_conv[0], w_pool_group[0].reshape(MATMUL_WEIGHTS[2][1]), w_o[0],
             w_up[0], w_down[0])

    state_pool_tm = jnp.transpose(state_pool[0], (1, 0, 2))
    y_sample, new_conv_sample, pool_tm, *w_bf16 = _sample_call(
        x_sample, state_conv, state_pool_tm, vectors, w_f32)
    new_pool_sample = jnp.transpose(pool_tm, (1, 0, 2))[None]

    wb_in, wb_out_conv, wb_pool, wb_o, wb_up, wb_down = w_bf16
    weights = (g_pre_mix, wb_in, b_gate, w_conv_flat, wb_out_conv, wb_pool, pool_scale, wb_o,
               g_post_mix, g_pre_mlp, wb_up, wb_down, g_post_mlp)
    y_prompt, new_conv_prompt, pool_tm = _prompt_call(x_prompt, weights)
    new_pool_prompt = jnp.transpose(pool_tm, (1, 0, 2))[None]
    return (y_prompt, y_sample, new_conv_prompt, new_pool_prompt, new_conv_sample,
            new_pool_sample)
```

```python
import jax
import jax.numpy as jnp
from jax.experimental import pallas as pl
from jax.experimental.pallas import tpu as pltpu

D_MODEL = 1024
D_FF = 4 * D_MODEL
CONV_WIDTH = 3
POOL_WINDOWS = (2, 4, 8, 16)
POOL_GROUP = D_MODEL // len(POOL_WINDOWS)
POOL_BUF = max(POOL_WINDOWS) - 1
PAST_LEN = 16384
EPS = 1e-6

CONV_HIST = 8
POOL_HIST = 16

PROMPT_TILE = 256
PROMPT_SEQS = 2
SAMPLE_SEQS = 32
FF_CHUNK = 1024
VMEM_LIMIT_BYTES = 58 * 1024 * 1024

MATMUL_WEIGHTS = (
    ("w_in", (D_MODEL, 6 * D_MODEL)),
    ("w_out_conv", (D_MODEL, D_MODEL)),
    ("w_pool_group", (len(POOL_WINDOWS) * POOL_GROUP, POOL_GROUP)),
    ("w_o", (D_MODEL, D_MODEL)),
    ("w_up", (D_MODEL, D_FF)),
    ("w_down", (D_FF, D_MODEL)),
)
STAGE_ROWS = 256
STAGE_COLS = 1024
STAGE_SLOTS = 6


def _rmsnorm(x, g):
    r = jax.lax.rsqrt(jnp.mean(x * x, axis=-1, keepdims=True) + EPS)
    return x * r * g


def _dot(a, w):
    return jnp.dot(a.astype(jnp.bfloat16), w, preferred_element_type=jnp.float32)


def _layer_stages(w):
    (g_pre_mix, w_in, b_gate, w_conv, w_out_conv, w_pool, pool_scale, w_o,
     g_post_mix, g_pre_mlp, w_up, w_down, g_post_mlp) = w
    d = D_MODEL

    def pre_norm(st):
        st["h"] = _rmsnorm(st["x"], g_pre_mix[...]).astype(jnp.bfloat16)

    def conv_branch(st):
        h = st["h"]
        xv = _dot(h, w_in[:, 0 * d:1 * d])
        cg = _dot(h, w_in[:, 2 * d:3 * d])
        bg = _dot(h, w_in[:, 1 * d:2 * d])
        z = cg * xv
        rows = z.shape[0]
        zext = jnp.concatenate([st["z_hist"], z], axis=0)
        taps = [w_conv[:, k * d:(k + 1) * d] for k in range(CONV_WIDTH)]
        if st["time_major"]:
            ns = st["nseq"]
            conv = zext[0:rows] * taps[0]
            for k in range(1, CONV_WIDTH):
                conv = conv + zext[k * ns:k * ns + rows] * taps[k]
        else:
            conv = zext * taps[CONV_WIDTH - 1]
            for k in range(1, CONV_WIDTH):
                conv = conv + pltpu.roll(zext, k, axis=0) * taps[CONV_WIDTH - 1 - k]
            conv = conv[CONV_HIST:]
        st["emit_z"](z)
        st["yc"] = (bg * conv).astype(jnp.bfloat16)

    def pool_branch(st):
        h = st["h"]
        u = _dot(h, w_in[:, 3 * d:4 * d])
        st["ya"] = _dot(st.pop("yc"), w_out_conv[...])
        rows = u.shape[0]
        uext = jnp.concatenate([st["u_hist"], u], axis=0)
        row = jax.lax.broadcasted_iota(jnp.int32, (rows, 1), 0)
        ns = st["nseq"]
        assert ns & (ns - 1) == 0, "time-major row -> time uses a shift"
        pos = st["start_pos"] + (row >> (ns.bit_length() - 1) if st["time_major"] else row)
        pooled = []
        for gi, win in enumerate(POOL_WINDOWS):
            sl = slice(gi * POOL_GROUP, (gi + 1) * POOL_GROUP)
            s = uext[:, sl]
            k = 1
            if st["time_major"]:
                lost = 0
                while k < win:
                    n = s.shape[0]
                    s = s[k * ns:] + s[:n - k * ns]
                    lost += k
                    k *= 2
                win_sum = s[(POOL_BUF - lost) * ns:]
            else:
                while k < win:
                    s = s + pltpu.roll(s, k, axis=0)
                    k *= 2
                win_sum = s[POOL_HIST:]
            inv_count = 1.0 / jnp.minimum(win, pos + 1).astype(jnp.float32)
            pooled.append((win_sum * inv_count - u[:, sl]).astype(jnp.bfloat16))
        st["emit_uext"](uext)
        st["pooled"] = pooled

    def gate_merge(st):
        h = st.pop("h")
        pooled = st.pop("pooled")
        bgate = b_gate[...]
        g_pool = jax.nn.sigmoid(_dot(h, w_in[:, 5 * d:6 * d]) + bgate[:, d:])
        yb = jnp.concatenate(
            [_dot(p, w_pool[gi * POOL_GROUP:(gi + 1) * POOL_GROUP, :])
             for gi, p in enumerate(pooled)], axis=-1)
        yb = yb * pool_scale[...]
        g_conv = jax.nn.sigmoid(_dot(h, w_in[:, 4 * d:5 * d]) + bgate[:, :d])
        st["merged"] = (g_conv * st.pop("ya") + g_pool * yb).astype(jnp.bfloat16)

    def out_proj(st):
        x1 = st["x"] + _rmsnorm(_dot(st.pop("merged"), w_o[...]), g_post_mix[...])
        st["x1"] = x1
        st["h2"] = _rmsnorm(x1, g_pre_mlp[...]).astype(jnp.bfloat16)

    def mlp_chunk(c):
        cs = slice(c * FF_CHUNK, (c + 1) * FF_CHUNK)

        def stage(st):
            a = jnp.square(jnp.maximum(_dot(st["h2"], w_up[:, cs]), 0.0))
            part = _dot(a, w_down[cs, :])
            st["f"] = part if c == 0 else st["f"] + part
        return stage

    def post_norm(st):
        st["y"] = st.pop("x1") + _rmsnorm(st.pop("f"), g_post_mlp[...])

    return ([pre_norm, conv_branch, pool_branch, gate_merge, out_proj]
            + [mlp_chunk(c) for c in range(D_FF // FF_CHUNK)] + [post_norm])


def _run_layer(tiles, w):
    stages = _layer_stages(w)
    for step in range(len(stages) + len(tiles) - 1):
        for i, st in enumerate(tiles):
            if 0 <= step - i < len(stages):
                stages[step - i](st)


def _prompt_kernel(x_hbm, *refs):
    w = refs[:13]
    y_hbm, conv_ref, pool_ref, xbuf, ybuf, z_tail, u_tail, in_sems, out_sems = refs[13:]
    _, n_seqs, t_len, _ = xbuf.shape
    n_tiles = x_hbm.shape[1] // t_len
    n_steps = (x_hbm.shape[0] // n_seqs) * n_tiles

    def hbm_tile(ref, g):
        b, j = g // n_tiles, g % n_tiles
        return ref.at[pl.ds(b * n_seqs, n_seqs), pl.ds(pl.multiple_of(j * t_len, t_len), t_len), :]

    def x_copy(g, slot):
        return pltpu.make_async_copy(hbm_tile(x_hbm, g), xbuf.at[slot], in_sems.at[slot])

    def y_copy(g, slot):
        return pltpu.make_async_copy(ybuf.at[slot], hbm_tile(y_hbm, g), out_sems.at[slot])

    def step(g, carry):
        slot = g % 2
        b, j = g // n_tiles, g % n_tiles

        @pl.when(g + 1 < n_steps)
        def _():
            x_copy(g + 1, 1 - slot).start()

        x_copy(g, slot).wait()

        @pl.when(g >= 2)
        def _():
            y_copy(g - 2, slot).wait()

        @pl.when(j == 0)
        def _():
            z_tail[...] = jnp.zeros_like(z_tail)
            u_tail[...] = jnp.zeros_like(u_tail)

        kept = [{} for _ in range(n_seqs)]
        tiles = [dict(x=xbuf[slot, s], z_hist=z_tail[s], u_hist=u_tail[s], time_major=False,
                      start_pos=j * t_len, nseq=1,
                      emit_z=lambda z, s=s: kept[s].update(z=z),
                      emit_uext=lambda uext, s=s: kept[s].update(uext=uext))
                 for s in range(n_seqs)]
        _run_layer(tiles, w)
        for s, st in enumerate(tiles):
            ybuf[slot, s] = st["y"]
            z, uext = kept[s]["z"], kept[s]["uext"]
            z_tail[s] = z[t_len - CONV_HIST:, :]
            u_tail[s] = uext[t_len:, :]
            conv_ref[0, b * n_seqs + s] = z[t_len - (CONV_WIDTH - 1):, :]
        y_copy(g, slot).start()

        @pl.when(j == n_tiles - 1)
        def _():
            for s in range(n_seqs):
                rows = u_tail[s, POOL_HIST - POOL_BUF:, :]
                pool_ref[:, pl.ds(b * n_seqs + s, 1), :] = rows.reshape(POOL_BUF, 1, rows.shape[-1])
        return carry

    x_copy(0, 0).start()
    jax.lax.fori_loop(0, n_steps, step, 0)
    for g in (n_steps - 2, n_steps - 1):
        y_copy(g, g % 2).wait()


def _weight_copy_jobs(w_hbm, w_bf16, stage, stage_sems, narrow_stage, narrow_sem):
    jobs = []
    n_ring = 0
    for k, (_, (n_rows, n_cols)) in enumerate(MATMUL_WEIGHTS):
        if n_cols % STAGE_COLS:
            assert narrow_stage.shape == (n_rows, n_cols)
            copy = pltpu.make_async_copy(w_hbm[k], narrow_stage, narrow_sem.at[0])
            jobs.append((copy, w_bf16[k], narrow_stage))
            continue
        for r0 in range(0, n_rows, STAGE_ROWS):
            for c0 in range(0, n_cols, STAGE_COLS):
                slot = n_ring % STAGE_SLOTS
                n_ring += 1
                window = (pl.ds(r0, STAGE_ROWS), pl.ds(c0, STAGE_COLS))
                copy = pltpu.make_async_copy(w_hbm[k].at[window], stage.at[slot],
                                             stage_sems.at[slot])
                jobs.append((copy, w_bf16[k].at[window], stage.at[slot]))
    return jobs


def _convert_weights(jobs):
    ahead = STAGE_SLOTS - 1
    for n, (copy, _, _) in enumerate(jobs[:ahead]):
        copy.start(priority=n % 2)
    for n, (copy, dst, staged) in enumerate(jobs):
        if n + ahead < len(jobs):
            jobs[n + ahead][0].start(priority=(n + ahead) % 2)
        copy.wait()
        dst[...] = staged[...].astype(jnp.bfloat16)


def _sample_kernel(x_ref, zs_ref, us_ref, g_pre_mix, b_gate, w_conv, pool_scale, g_post_mix,
                   g_pre_mlp, g_post_mlp, *refs):
    n_w = len(MATMUL_WEIGHTS)
    w_hbm = refs[:n_w]
    y_ref, conv_ref, pool_ref = refs[n_w:n_w + 3]
    w_out_hbm = refs[n_w + 3:2 * n_w + 3]
    w_bf16 = refs[2 * n_w + 3:3 * n_w + 3]
    stage, narrow_stage, stage_sems, narrow_sem, out_sems = refs[3 * n_w + 3:]
    i = pl.program_id(0)
    w_in, w_out_conv, w_pool, w_o, w_up, w_down = w_bf16

    def out_copy(k):
        return pltpu.make_async_copy(w_bf16[k], w_out_hbm[k], out_sems.at[k])

    @pl.when(i == 0)
    def _():
        _convert_weights(
            _weight_copy_jobs(w_hbm, w_bf16, stage, stage_sems, narrow_stage, narrow_sem))
        for k in range(n_w):
            out_copy(k).start()

    nseq, t_len, _ = x_ref.shape
    x = jnp.concatenate([x_ref[:, t, :] for t in range(t_len)], axis=0)
    z_hist = jnp.concatenate([zs_ref[:, k, :] for k in range(CONV_WIDTH - 1)], axis=0)
    u_hist = us_ref[...].reshape(POOL_BUF * nseq, D_MODEL)

    def emit_z(z):
        for k in range(CONV_WIDTH - 1):
            t = t_len - (CONV_WIDTH - 1) + k
            conv_ref[:, k, :] = z[t * nseq:(t + 1) * nseq]

    def emit_uext(uext):
        pool_ref[...] = uext[t_len * nseq:].reshape(POOL_BUF, nseq, D_MODEL)

    st = dict(x=x, z_hist=z_hist, u_hist=u_hist, time_major=True, start_pos=PAST_LEN, nseq=nseq,
              emit_z=emit_z, emit_uext=emit_uext)
    w = (g_pre_mix, w_in, b_gate, w_conv, w_out_conv, w_pool, pool_scale, w_o,
         g_post_mix, g_pre_mlp, w_up, w_down, g_post_mlp)
    _run_layer([st], w)
    y = st["y"]
    for t in range(t_len):
        y_ref[:, t, :] = y[t * nseq:(t + 1) * nseq]

    @pl.when(i == pl.num_programs(0) - 1)
    def _():
        for k in range(n_w):
            out_copy(k).wait()


def _const_spec(shape):
    zeros = (0,) * len(shape)
    return pl.BlockSpec(shape, lambda i: zeros, pipeline_mode=pl.Buffered(1))


def _prompt_call(x, weights):
    batch, seq, d = x.shape
    tm = PROMPT_TILE
    nb = PROMPT_SEQS
    f32 = jnp.float32
    vmem = pl.BlockSpec(memory_space=pltpu.VMEM)
    hbm = pl.BlockSpec(memory_space=pl.ANY)
    return pl.pallas_call(
        _prompt_kernel,
        in_specs=[hbm] + [vmem] * len(weights),
        out_specs=[hbm, vmem, vmem],
        out_shape=[
            jax.ShapeDtypeStruct((batch, seq, d), f32),
            jax.ShapeDtypeStruct((1, batch, CONV_WIDTH - 1, d), f32),
            jax.ShapeDtypeStruct((POOL_BUF, batch, d), f32),
        ],
        scratch_shapes=[pltpu.VMEM((2, nb, tm, d), f32),
                        pltpu.VMEM((2, nb, tm, d), f32),
                        pltpu.VMEM((nb, CONV_HIST, d), f32),
                        pltpu.VMEM((nb, POOL_HIST, d), f32),
                        pltpu.SemaphoreType.DMA((2,)),
                        pltpu.SemaphoreType.DMA((2,))],
        compiler_params=pltpu.CompilerParams(vmem_limit_bytes=VMEM_LIMIT_BYTES),
        name="prompt_layer",
    )(x, *weights)


def _sample_call(x, state_conv, state_pool_tm, vectors, w_f32):
    nseq_total, t_len, d = x.shape
    assert CONV_WIDTH - 1 <= t_len <= POOL_BUF
    ns = SAMPLE_SEQS
    f32, bf16 = jnp.float32, jnp.bfloat16
    any_spec = pl.BlockSpec(memory_space=pl.ANY)
    w_shapes = [shape for _, shape in MATMUL_WEIGHTS]
    narrow_shapes = [shape for shape in w_shapes if shape[1] % STAGE_COLS]
    assert len(narrow_shapes) == 1
    return pl.pallas_call(
        _sample_kernel,
        grid=(nseq_total // ns,),
        in_specs=([
            pl.BlockSpec((ns, t_len, d), lambda i: (i, 0, 0)),
            pl.BlockSpec((None, ns, CONV_WIDTH - 1, d), lambda i: (0, i, 0, 0)),
            pl.BlockSpec((POOL_BUF, ns, d), lambda i: (0, i, 0)),
        ] + [_const_spec(v.shape) for v in vectors] + [any_spec] * len(w_f32)),
        out_specs=[
            pl.BlockSpec((ns, t_len, d), lambda i: (i, 0, 0)),
            pl.BlockSpec((None, ns, CONV_WIDTH - 1, d), lambda i: (0, i, 0, 0)),
            pl.BlockSpec((POOL_BUF, ns, d), lambda i: (0, i, 0)),
        ] + [any_spec] * len(w_f32),
        out_shape=[
            jax.ShapeDtypeStruct((nseq_total, t_len, d), f32),
            jax.ShapeDtypeStruct((1, nseq_total, CONV_WIDTH - 1, d), f32),
            jax.ShapeDtypeStruct((POOL_BUF, nseq_total, d), f32),
        ] + [jax.ShapeDtypeStruct(s, bf16) for s in w_shapes],
        scratch_shapes=([pltpu.VMEM(s, bf16) for s in w_shapes]
                        + [pltpu.VMEM((STAGE_SLOTS, STAGE_ROWS, STAGE_COLS), f32),
                           pltpu.VMEM(narrow_shapes[0], f32),
                           pltpu.SemaphoreType.DMA((STAGE_SLOTS,)),
                           pltpu.SemaphoreType.DMA((1,)),
                           pltpu.SemaphoreType.DMA((len(w_f32),))]),
        compiler_params=pltpu.CompilerParams(
            dimension_semantics=("arbitrary",),
            vmem_limit_bytes=VMEM_LIMIT_BYTES),
        name="sample_layer",
    )(x, state_conv, state_pool_tm, *vectors, *w_f32)


def kernel(x_prompt, x_sample, state_conv, state_pool, g_pre_mix, w_in, b_gate, w_conv,
           w_out_conv, w_pool_group, pool_scale, w_o, g_post_mix, g_pre_mlp, w_up, w_down,
           g_post_mlp):
    depth = w_in.shape[0]
    assert depth == 1, "single-layer step"
    d = D_MODEL
    w_conv_flat = w_conv.reshape(1, CONV_WIDTH * d)
    vectors = (g_pre_mix, b_gate, w_conv_flat, pool_scale, g_post_mix, g_pre_mlp, g_post_mlp)
    w_f32 = (w_in[0], w_out_conv[0], w_pool_group[0].reshape(MATMUL_WEIGHTS[2][1]), w_o[0],
             w_up[0], w_down[0])

    state_pool_tm = jnp.transpose(state_pool[0], (1, 0, 2))
    y_sample, new_conv_sample, pool_tm, *w_bf16 = _sample_call(
        x_sample, state_conv, state_pool_tm, vectors, w_f32)
    new_pool_sample = jnp.transpose(pool_tm, (1, 0, 2))[None]

    wb_in, wb_out_conv, wb_pool, wb_o, wb_up, wb_down = w_bf16
    weights = (g_pre_mix, wb_in, b_gate, w_conv_flat, wb_out_conv, wb_pool, pool_scale, wb_o,
               g_post_mix, g_pre_mlp, wb_up, wb_down, g_post_mlp)
    y_prompt, new_conv_prompt, pool_tm = _prompt_call(x_prompt, weights)
    new_pool_prompt = jnp.transpose(pool_tm, (1, 0, 2))[None]
    return (y_prompt, y_sample, new_conv_prompt, new_pool_prompt, new_conv_sample,
            new_pool_sample)
```

```python
import jax
import jax.numpy as jnp
from jax.experimental import pallas as pl
from jax.experimental.pallas import tpu as pltpu

D_MODEL = 1024
D_FF = 4 * D_MODEL
CONV_WIDTH = 3
POOL_WINDOWS = (2, 4, 8, 16)
POOL_GROUP = D_MODEL // len(POOL_WINDOWS)
POOL_BUF = max(POOL_WINDOWS) - 1
PAST_LEN = 16384
EPS = 1e-6

CONV_HIST = 8
POOL_HIST = 16

PROMPT_TILE = 256
PROMPT_SEQS = 2
SAMPLE_SEQS = 32
FF_CHUNK = 1024
VMEM_LIMIT_BYTES = 58 * 1024 * 1024

MATMUL_WEIGHTS = (
    ("w_in", (D_MODEL, 6 * D_MODEL)),
    ("w_out_conv", (D_MODEL, D_MODEL)),
    ("w_pool_group", (len(POOL_WINDOWS) * POOL_GROUP, POOL_GROUP)),
    ("w_o", (D_MODEL, D_MODEL)),
    ("w_up", (D_MODEL, D_FF)),
    ("w_down", (D_FF, D_MODEL)),
)
STAGE_ROWS = 256
STAGE_COLS = 1024
STAGE_SLOTS = 6


def _rmsnorm(x, g):
    r = jax.lax.rsqrt(jnp.mean(x * x, axis=-1, keepdims=True) + EPS)
    return x * r * g


def _dot(a, w):
    return jnp.dot(a.astype(jnp.bfloat16), w, preferred_element_type=jnp.float32)


def _layer_stages(w):
    (g_pre_mix, w_in, b_gate, w_conv, w_out_conv, w_pool, pool_scale, w_o,
     g_post_mix, g_pre_mlp, w_up, w_down, g_post_mlp) = w
    d = D_MODEL

    def pre_norm(st):
        st["h"] = _rmsnorm(st["x"], g_pre_mix[...]).astype(jnp.bfloat16)

    def conv_branch(st):
        h = st["h"]
        xv = _dot(h, w_in[:, 0 * d:1 * d])
        cg = _dot(h, w_in[:, 2 * d:3 * d])
        bg = _dot(h, w_in[:, 1 * d:2 * d])
        z = cg * xv
        rows = z.shape[0]
        zext = jnp.concatenate([st["z_hist"], z], axis=0)
        taps = [w_conv[:, k * d:(k + 1) * d] for k in range(CONV_WIDTH)]
        if st["time_major"]:
            ns = st["nseq"]
            conv = zext[0:rows] * taps[0]
            for k in range(1, CONV_WIDTH):
                conv = conv + zext[k * ns:k * ns + rows] * taps[k]
        else:
            conv = zext * taps[CONV_WIDTH - 1]
            for k in range(1, CONV_WIDTH):
                conv = conv + pltpu.roll(zext, k, axis=0) * taps[CONV_WIDTH - 1 - k]
            conv = conv[CONV_HIST:]
        st["emit_z"](z)
        st["yc"] = (bg * conv).astype(jnp.bfloat16)

    def pool_branch(st):
        h = st["h"]
        u = _dot(h, w_in[:, 3 * d:4 * d])
        st["ya"] = _dot(st.pop("yc"), w_out_conv[...])
        rows = u.shape[0]
        uext = jnp.concatenate([st["u_hist"], u], axis=0)
        row = jax.lax.broadcasted_iota(jnp.int32, (rows, 1), 0)
        ns = st["nseq"]
        assert ns & (ns - 1) == 0, "time-major row -> time uses a shift"
        pos = st["start_pos"] + (row >> (ns.bit_length() - 1) if st["time_major"] else row)
        pooled = []
        for gi, win in enumerate(POOL_WINDOWS):
            sl = slice(gi * POOL_GROUP, (gi + 1) * POOL_GROUP)
            s = uext[:, sl]
            k = 1
            if st["time_major"]:
                lost = 0
                while k < win:
                    n = s.shape[0]
                    s = s[k * ns:] + s[:n - k * ns]
                    lost += k
                    k *= 2
                win_sum = s[(POOL_BUF - lost) * ns:]
            else:
                while k < win:
                    s = s + pltpu.roll(s, k, axis=0)
                    k *= 2
                win_sum = s[POOL_HIST:]
            inv_count = 1.0 / jnp.minimum(win, pos + 1).astype(jnp.float32)
            pooled.append((win_sum * inv_count - u[:, sl]).astype(jnp.bfloat16))
        st["emit_uext"](uext)
        st["pooled"] = pooled

    def gate_merge(st):
        h = st.pop("h")
        pooled = st.pop("pooled")
        bgate = b_gate[...]
        g_pool = jax.nn.sigmoid(_dot(h, w_in[:, 5 * d:6 * d]) + bgate[:, d:])
        yb = jnp.concatenate(
            [_dot(p, w_pool[gi * POOL_GROUP:(gi + 1) * POOL_GROUP, :])
             for gi, p in enumerate(pooled)], axis=-1)
        yb = yb * pool_scale[...]
        g_conv = jax.nn.sigmoid(_dot(h, w_in[:, 4 * d:5 * d]) + bgate[:, :d])
        st["merged"] = (g_conv * st.pop("ya") + g_pool * yb).astype(jnp.bfloat16)

    def out_proj(st):
        x1 = st["x"] + _rmsnorm(_dot(st.pop("merged"), w_o[...]), g_post_mix[...])
        st["x1"] = x1
        st["h2"] = _rmsnorm(x1, g_pre_mlp[...]).astype(jnp.bfloat16)

    def mlp_chunk(c):
        cs = slice(c * FF_CHUNK, (c + 1) * FF_CHUNK)

        def stage(st):
            a = jnp.square(jnp.maximum(_dot(st["h2"], w_up[:, cs]), 0.0))
            part = _dot(a, w_down[cs, :])
            st["f"] = part if c == 0 else st["f"] + part
        return stage

    def post_norm(st):
        st["y"] = st.pop("x1") + _rmsnorm(st.pop("f"), g_post_mlp[...])

    return ([pre_norm, conv_branch, pool_branch, gate_merge, out_proj]
            + [mlp_chunk(c) for c in range(D_FF // FF_CHUNK)] + [post_norm])


def _run_layer(tiles, w):
    stages = _layer_stages(w)
    for step in range(len(stages) + len(tiles) - 1):
        for i, st in enumerate(tiles):
            if 0 <= step - i < len(stages):
                stages[step - i](st)


def _prompt_kernel(x_hbm, *refs):
    w = refs[:13]
    y_hbm, conv_ref, pool_ref, xbuf, ybuf, z_tail, u_tail, in_sems, out_sems = refs[13:]
    _, n_seqs, t_len, _ = xbuf.shape
    n_tiles = x_hbm.shape[1] // t_len
    n_steps = (x_hbm.shape[0] // n_seqs) * n_tiles

    def hbm_tile(ref, g):
        b, j = g // n_tiles, g % n_tiles
        return ref.at[pl.ds(b * n_seqs, n_seqs), pl.ds(pl.multiple_of(j * t_len, t_len), t_len), :]

    def x_copy(g, slot):
        return pltpu.make_async_copy(hbm_tile(x_hbm, g), xbuf.at[slot], in_sems.at[slot])

    def y_copy(g, slot):
        return pltpu.make_async_copy(ybuf.at[slot], hbm_tile(y_hbm, g), out_sems.at[slot])

    def step(g, carry):
        slot = g % 2
        b, j = g // n_tiles, g % n_tiles

        @pl.when(g + 1 < n_steps)
        def _():
            x_copy(g + 1, 1 - slot).start()

        x_copy(g, slot).wait()

        @pl.when(g >= 2)
        def _():
            y_copy(g - 2, slot).wait()

        @pl.when(j == 0)
        def _():
            z_tail[...] = jnp.zeros_like(z_tail)
            u_tail[...] = jnp.zeros_like(u_tail)

        kept = [{} for _ in range(n_seqs)]
        tiles = [dict(x=xbuf[slot, s], z_hist=z_tail[s], u_hist=u_tail[s], time_major=False,
                      start_pos=j * t_len, nseq=1,
                      emit_z=lambda z, s=s: kept[s].update(z=z),
                      emit_uext=lambda uext, s=s: kept[s].update(uext=uext))
                 for s in range(n_seqs)]
        _run_layer(tiles, w)
        for s, st in enumerate(tiles):
            ybuf[slot, s] = st["y"]
            z, uext = kept[s]["z"], kept[s]["uext"]
            z_tail[s] = z[t_len - CONV_HIST:, :]
            u_tail[s] = uext[t_len:, :]
            conv_ref[0, b * n_seqs + s] = z[t_len - (CONV_WIDTH - 1):, :]
        y_copy(g, slot).start(priority=1)

        @pl.when(j == n_tiles - 1)
        def _():
            for s in range(n_seqs):
                rows = u_tail[s, POOL_HIST - POOL_BUF:, :]
                pool_ref[:, pl.ds(b * n_seqs + s, 1), :] = rows.reshape(POOL_BUF, 1, rows.shape[-1])
        return carry

    x_copy(0, 0).start()
    jax.lax.fori_loop(0, n_steps, step, 0)
    for g in (n_steps - 2, n_steps - 1):
        y_copy(g, g % 2).wait()


def _weight_copy_jobs(w_hbm, w_bf16, stage, stage_sems, narrow_stage, narrow_sem):
    jobs = []
    n_ring = 0
    for k, (_, (n_rows, n_cols)) in enumerate(MATMUL_WEIGHTS):
        if n_cols % STAGE_COLS:
            assert narrow_stage.shape == (n_rows, n_cols)
            copy = pltpu.make_async_copy(w_hbm[k], narrow_stage, narrow_sem.at[0])
            jobs.append((copy, w_bf16[k], narrow_stage))
            continue
        for r0 in range(0, n_rows, STAGE_ROWS):
            for c0 in range(0, n_cols, STAGE_COLS):
                slot = n_ring % STAGE_SLOTS
                n_ring += 1
                window = (pl.ds(r0, STAGE_ROWS), pl.ds(c0, STAGE_COLS))
                copy = pltpu.make_async_copy(w_hbm[k].at[window], stage.at[slot],
                                             stage_sems.at[slot])
                jobs.append((copy, w_bf16[k].at[window], stage.at[slot]))
    return jobs


def _convert_weights(jobs):
    ahead = STAGE_SLOTS - 1
    for copy, _, _ in jobs[:ahead]:
        copy.start()
    for n, (copy, dst, staged) in enumerate(jobs):
        if n + ahead < len(jobs):
            jobs[n + ahead][0].start()
        copy.wait()
        dst[...] = staged[...].astype(jnp.bfloat16)


def _sample_kernel(x_ref, zs_ref, us_ref, g_pre_mix, b_gate, w_conv, pool_scale, g_post_mix,
                   g_pre_mlp, g_post_mlp, *refs):
    n_w = len(MATMUL_WEIGHTS)
    w_hbm = refs[:n_w]
    y_ref, conv_ref, pool_ref = refs[n_w:n_w + 3]
    w_out_hbm = refs[n_w + 3:2 * n_w + 3]
    w_bf16 = refs[2 * n_w + 3:3 * n_w + 3]
    stage, narrow_stage, stage_sems, narrow_sem, out_sems = refs[3 * n_w + 3:]
    i = pl.program_id(0)
    w_in, w_out_conv, w_pool, w_o, w_up, w_down = w_bf16

    def out_copy(k):
        return pltpu.make_async_copy(w_bf16[k], w_out_hbm[k], out_sems.at[k])

    @pl.when(i == 0)
    def _():
        _convert_weights(
            _weight_copy_jobs(w_hbm, w_bf16, stage, stage_sems, narrow_stage, narrow_sem))
        for k in range(n_w):
            out_copy(k).start()

    nseq, t_len, _ = x_ref.shape
    x = jnp.concatenate([x_ref[:, t, :] for t in range(t_len)], axis=0)
    z_hist = jnp.concatenate([zs_ref[:, k, :] for k in range(CONV_WIDTH - 1)], axis=0)
    u_hist = us_ref[...].reshape(POOL_BUF * nseq, D_MODEL)

    def emit_z(z):
        for k in range(CONV_WIDTH - 1):
            t = t_len - (CONV_WIDTH - 1) + k
            conv_ref[:, k, :] = z[t * nseq:(t + 1) * nseq]

    def emit_uext(uext):
        pool_ref[...] = uext[t_len * nseq:].reshape(POOL_BUF, nseq, D_MODEL)

    st = dict(x=x, z_hist=z_hist, u_hist=u_hist, time_major=True, start_pos=PAST_LEN, nseq=nseq,
              emit_z=emit_z, emit_uext=emit_uext)
    w = (g_pre_mix, w_in, b_gate, w_conv, w_out_conv, w_pool, pool_scale, w_o,
         g_post_mix, g_pre_mlp, w_up, w_down, g_post_mlp)
    _run_layer([st], w)
    y = st["y"]
    for t in range(t_len):
        y_ref[:, t, :] = y[t * nseq:(t + 1) * nseq]

    @pl.when(i == pl.num_programs(0) - 1)
    def _():
        for k in range(n_w):
            out_copy(k).wait()


def _const_spec(shape):
    zeros = (0,) * len(shape)
    return pl.BlockSpec(shape, lambda i: zeros, pipeline_mode=pl.Buffered(1))


def _prompt_call(x, weights):
    batch, seq, d = x.shape
    tm = PROMPT_TILE
    nb = PROMPT_SEQS
    f32 = jnp.float32
    vmem = pl.BlockSpec(memory_space=pltpu.VMEM)
    hbm = pl.BlockSpec(memory_space=pl.ANY)
    return pl.pallas_call(
        _prompt_kernel,
        in_specs=[hbm] + [vmem] * len(weights),
        out_specs=[hbm, vmem, vmem],
        out_shape=[
            jax.ShapeDtypeStruct((batch, seq, d), f32),
            jax.ShapeDtypeStruct((1, batch, CONV_WIDTH - 1, d), f32),
            jax.ShapeDtypeStruct((POOL_BUF, batch, d), f32),
        ],
        scratch_shapes=[pltpu.VMEM((2, nb, tm, d), f32),
                        pltpu.VMEM((2, nb, tm, d), f32),
                        pltpu.VMEM((nb, CONV_HIST, d), f32),
                        pltpu.VMEM((nb, POOL_HIST, d), f32),
                        pltpu.SemaphoreType.DMA((2,)),
                        pltpu.SemaphoreType.DMA((2,))],
        compiler_params=pltpu.CompilerParams(vmem_limit_bytes=VMEM_LIMIT_BYTES),
        name="prompt_layer",
    )(x, *weights)


def _sample_call(x, state_conv, state_pool_tm, vectors, w_f32):
    nseq_total, t_len, d = x.shape
    assert CONV_WIDTH - 1 <= t_len <= POOL_BUF
    ns = SAMPLE_SEQS
    f32, bf16 = jnp.float32, jnp.bfloat16
    any_spec = pl.BlockSpec(memory_space=pl.ANY)
    w_shapes = [shape for _, shape in MATMUL_WEIGHTS]
    narrow_shapes = [shape for shape in w_shapes if shape[1] % STAGE_COLS]
    assert len(narrow_shapes) == 1
    return pl.pallas_call(
        _sample_kernel,
        grid=(nseq_total // ns,),
        in_specs=([
            pl.BlockSpec((ns, t_len, d), lambda i: (i, 0, 0)),
            pl.BlockSpec((None, ns, CONV_WIDTH - 1, d), lambda i: (0, i, 0, 0)),
            pl.BlockSpec((POOL_BUF, ns, d), lambda i: (0, i, 0)),
        ] + [_const_spec(v.shape) for v in vectors] + [any_spec] * len(w_f32)),
        out_specs=[
            pl.BlockSpec((ns, t_len, d), lambda i: (i, 0, 0)),
            pl.BlockSpec((None, ns, CONV_WIDTH - 1, d), lambda i: (0, i, 0, 0)),
            pl.BlockSpec((POOL_BUF, ns, d), lambda i: (0, i, 0)),
        ] + [any_spec] * len(w_f32),
        out_shape=[
            jax.ShapeDtypeStruct((nseq_total, t_len, d), f32),
            jax.ShapeDtypeStruct((1, nseq_total, CONV_WIDTH - 1, d), f32),
            jax.ShapeDtypeStruct((POOL_BUF, nseq_total, d), f32),
        ] + [jax.ShapeDtypeStruct(s, bf16) for s in w_shapes],
        scratch_shapes=([pltpu.VMEM(s, bf16) for s in w_shapes]
                        + [pltpu.VMEM((STAGE_SLOTS, STAGE_ROWS, STAGE_COLS), f32),
                           pltpu.VMEM(narrow_shapes[0], f32),
                           pltpu.SemaphoreType.DMA((STAGE_SLOTS,)),
                           pltpu.SemaphoreType.DMA((1,)),
                           pltpu.SemaphoreType.DMA((len(w_f32),))]),
        compiler_params=pltpu.CompilerParams(
            dimension_semantics=("arbitrary",),
            vmem_limit_bytes=VMEM_LIMIT_BYTES),
        name="sample_layer",
    )(x, state_conv, state_pool_tm, *vectors, *w_f32)


def kernel(x_prompt, x_sample, state_conv, state_pool, g_pre_mix, w_in, b_gate, w_conv,
           w_out_conv, w_pool_group, pool_scale, w_o, g_post_mix, g_pre_mlp, w_up, w_down,
           g_post_mlp):
    depth = w_in.shape[0]
    assert depth == 1, "single-layer step"
    d = D_MODEL
    w_conv_flat = w_conv.reshape(1, CONV_WIDTH * d)
    vectors = (g_pre_mix, b_gate, w_conv_flat, pool_scale, g_post_mix, g_pre_mlp, g_post_mlp)
    w_f32 = (w_in[0], w_out_conv[0], w_pool_group[0].reshape(MATMUL_WEIGHTS[2][1]), w_o[0],
             w_up[0], w_down[0])

    state_pool_tm = jnp.transpose(state_pool[0], (1, 0, 2))
    y_sample, new_conv_sample, pool_tm, *w_bf16 = _sample_call(
        x_sample, state_conv, state_pool_tm, vectors, w_f32)
    new_pool_sample = jnp.transpose(pool_tm, (1, 0, 2))[None]

    wb_in, wb_out_conv, wb_pool, wb_o, wb_up, wb_down = w_bf16
    weights = (g_pre_mix, wb_in, b_gate, w_conv_flat, wb_out_conv, wb_pool, pool_scale, wb_o,
               g_post_mix, g_pre_mlp, wb_up, wb_down, g_post_mlp)
    y_prompt, new_conv_prompt, pool_tm = _prompt_call(x_prompt, weights)
    new_pool_prompt = jnp.transpose(pool_tm, (1, 0, 2))[None]
    return (y_prompt, y_sample, new_conv_prompt, new_pool_prompt, new_conv_sample,
            new_pool_sample)
```

```python
import jax
import jax.numpy as jnp
from jax.experimental import pallas as pl
from jax.experimental.pallas import tpu as pltpu

D_MODEL = 1024
D_FF = 4 * D_MODEL
CONV_WIDTH = 3
POOL_WINDOWS = (2, 4, 8, 16)
POOL_GROUP = D_MODEL // len(POOL_WINDOWS)
POOL_BUF = max(POOL_WINDOWS) - 1
PAST_LEN = 16384
EPS = 1e-6

CONV_HIST = 8
POOL_HIST = 16

PROMPT_TILE = 256
PROMPT_SEQS = 2
SAMPLE_SEQS = 32
FF_CHUNK = 1024
VMEM_LIMIT_BYTES = 58 * 1024 * 1024

MATMUL_WEIGHTS = (
    ("w_in", (D_MODEL, 6 * D_MODEL)),
    ("w_out_conv", (D_MODEL, D_MODEL)),
    ("w_pool_group", (len(POOL_WINDOWS) * POOL_GROUP, POOL_GROUP)),
    ("w_o", (D_MODEL, D_MODEL)),
    ("w_up", (D_MODEL, D_FF)),
    ("w_down", (D_FF, D_MODEL)),
)
STAGE_ROWS = 256
STAGE_COLS = 1024
STAGE_SLOTS = 6


def _rmsnorm(x, g):
    r = jax.lax.rsqrt(jnp.mean(x * x, axis=-1, keepdims=True) + EPS)
    return x * r * g


def _dot(a, w):
    return jnp.dot(a.astype(jnp.bfloat16), w, preferred_element_type=jnp.float32)


def _layer_stages(w):
    (g_pre_mix, w_in, b_gate, w_conv, w_out_conv, w_pool, pool_scale, w_o,
     g_post_mix, g_pre_mlp, w_up, w_down, g_post_mlp) = w
    d = D_MODEL

    def pre_norm(st):
        st["h"] = _rmsnorm(st["x"], g_pre_mix[...]).astype(jnp.bfloat16)

    def conv_branch(st):
        h = st["h"]
        xv = _dot(h, w_in[:, 0 * d:1 * d])
        cg = _dot(h, w_in[:, 2 * d:3 * d])
        bg = _dot(h, w_in[:, 1 * d:2 * d])
        z = cg * xv
        rows = z.shape[0]
        zext = jnp.concatenate([st["z_hist"], z], axis=0)
        taps = [w_conv[:, k * d:(k + 1) * d] for k in range(CONV_WIDTH)]
        if st["time_major"]:
            ns = st["nseq"]
            conv = zext[0:rows] * taps[0]
            for k in range(1, CONV_WIDTH):
                conv = conv + zext[k * ns:k * ns + rows] * taps[k]
        else:
            conv = zext * taps[CONV_WIDTH - 1]
            for k in range(1, CONV_WIDTH):
                conv = conv + pltpu.roll(zext, k, axis=0) * taps[CONV_WIDTH - 1 - k]
            conv = conv[CONV_HIST:]
        st["emit_z"](z)
        st["yc"] = (bg * conv).astype(jnp.bfloat16)

    def pool_branch(st):
        h = st["h"]
        u = _dot(h, w_in[:, 3 * d:4 * d])
        st["ya"] = _dot(st.pop("yc"), w_out_conv[...])
        rows = u.shape[0]
        uext = jnp.concatenate([st["u_hist"], u], axis=0)
        row = jax.lax.broadcasted_iota(jnp.int32, (rows, 1), 0)
        ns = st["nseq"]
        assert ns & (ns - 1) == 0, "time-major row -> time uses a shift"
        pos = st["start_pos"] + (row >> (ns.bit_length() - 1) if st["time_major"] else row)
        pooled = []
        for gi, win in enumerate(POOL_WINDOWS):
            sl = slice(gi * POOL_GROUP, (gi + 1) * POOL_GROUP)
            s = uext[:, sl]
            k = 1
            if st["time_major"]:
                lost = 0
                while k < win:
                    n = s.shape[0]
                    s = s[k * ns:] + s[:n - k * ns]
                    lost += k
                    k *= 2
                win_sum = s[(POOL_BUF - lost) * ns:]
            else:
                while k < win:
                    s = s + pltpu.roll(s, k, axis=0)
                    k *= 2
                win_sum = s[POOL_HIST:]
            inv_count = 1.0 / jnp.minimum(win, pos + 1).astype(jnp.float32)
            pooled.append((win_sum * inv_count - u[:, sl]).astype(jnp.bfloat16))
        st["emit_uext"](uext)
        st["pooled"] = pooled

    def gate_merge(st):
        h = st.pop("h")
        pooled = st.pop("pooled")
        bgate = b_gate[...]
        g_pool = jax.nn.sigmoid(_dot(h, w_in[:, 5 * d:6 * d]) + bgate[:, d:])
        yb = jnp.concatenate(
            [_dot(p, w_pool[gi * POOL_GROUP:(gi + 1) * POOL_GROUP, :])
             for gi, p in enumerate(pooled)], axis=-1)
        yb = yb * pool_scale[...]
        g_conv = jax.nn.sigmoid(_dot(h, w_in[:, 4 * d:5 * d]) + bgate[:, :d])
        st["merged"] = (g_conv * st.pop("ya") + g_pool * yb).astype(jnp.bfloat16)

    def out_proj(st):
        x1 = st["x"] + _rmsnorm(_dot(st.pop("merged"), w_o[...]), g_post_mix[...])
        st["x1"] = x1
        st["h2"] = _rmsnorm(x1, g_pre_mlp[...]).astype(jnp.bfloat16)

    def mlp_chunk(c):
        cs = slice(c * FF_CHUNK, (c + 1) * FF_CHUNK)

        def stage(st):
            a = jnp.square(jnp.maximum(_dot(st["h2"], w_up[:, cs]), 0.0))
            part = _dot(a, w_down[cs, :])
            st["f"] = part if c == 0 else st["f"] + part
        return stage

    def post_norm(st):
        st["y"] = st.pop("x1") + _rmsnorm(st.pop("f"), g_post_mlp[...])

    return ([pre_norm, conv_branch, pool_branch, gate_merge, out_proj]
            + [mlp_chunk(c) for c in range(D_FF // FF_CHUNK)] + [post_norm])


def _run_layer(tiles, w):
    stages = _layer_stages(w)
    for step in range(len(stages) + len(tiles) - 1):
        for i, st in enumerate(tiles):
            if 0 <= step - i < len(stages):
                stages[step - i](st)


def _prompt_kernel(x_hbm, *refs):
    w = refs[:13]
    y_hbm, conv_ref, pool_ref, xbuf, ybuf, z_tail, u_tail, in_sems, out_sems = refs[13:]
    _, n_seqs, t_len, _ = xbuf.shape
    n_tiles = x_hbm.shape[1] // t_len
    n_steps = (x_hbm.shape[0] // n_seqs) * n_tiles

    def hbm_tile(ref, g):
        b, j = g // n_tiles, g % n_tiles
        return ref.at[pl.ds(b * n_seqs, n_seqs), pl.ds(pl.multiple_of(j * t_len, t_len), t_len), :]

    def x_copy(g, slot):
        return pltpu.make_async_copy(hbm_tile(x_hbm, g), xbuf.at[slot], in_sems.at[slot])

    def y_copy(g, slot):
        return pltpu.make_async_copy(ybuf.at[slot], hbm_tile(y_hbm, g), out_sems.at[slot])

    def step(g, carry):
        slot = g % 2
        b, j = g // n_tiles, g % n_tiles

        @pl.when(g + 1 < n_steps)
        def _():
            x_copy(g + 1, 1 - slot).start()

        x_copy(g, slot).wait()

        @pl.when(g >= 2)
        def _():
            y_copy(g - 2, slot).wait()

        @pl.when(j == 0)
        def _():
            z_tail[...] = jnp.zeros_like(z_tail)
            u_tail[...] = jnp.zeros_like(u_tail)

        kept = [{} for _ in range(n_seqs)]
        tiles = [dict(x=xbuf[slot, s], z_hist=z_tail[s], u_hist=u_tail[s], time_major=False,
                      start_pos=j * t_len, nseq=1,
                      emit_z=lambda z, s=s: kept[s].update(z=z),
                      emit_uext=lambda uext, s=s: kept[s].update(uext=uext))
                 for s in range(n_seqs)]
        _run_layer(tiles, w)
        for s, st in enumerate(tiles):
            ybuf[slot, s] = st["y"]
            z, uext = kept[s]["z"], kept[s]["uext"]
            z_tail[s] = z[t_len - CONV_HIST:, :]
            u_tail[s] = uext[t_len:, :]
            conv_ref[0, b * n_seqs + s] = z[t_len - (CONV_WIDTH - 1):, :]
        y_copy(g, slot).start(priority=1)

        @pl.when(j == n_tiles - 1)
        def _():
            for s in range(n_seqs):
                rows = u_tail[s, POOL_HIST - POOL_BUF:, :]
                pool_ref[:, pl.ds(b * n_seqs + s, 1), :] = rows.reshape(POOL_BUF, 1, rows.shape[-1])
        return carry

    x_copy(0, 0).start()
    jax.lax.fori_loop(0, n_steps, step, 0)
    for g in (n_steps - 2, n_steps - 1):
        y_copy(g, g % 2).wait()


def _weight_copy_jobs(w_hbm, w_bf16, stage, stage_sems, narrow_stage, narrow_sem):
    jobs = []
    n_ring = 0
    for k, (_, (n_rows, n_cols)) in enumerate(MATMUL_WEIGHTS):
        if n_cols % STAGE_COLS:
            assert narrow_stage.shape == (n_rows, n_cols)
            copy = pltpu.make_async_copy(w_hbm[k], narrow_stage, narrow_sem.at[0])
            jobs.append((copy, w_bf16[k], narrow_stage))
            continue
        for r0 in range(0, n_rows, STAGE_ROWS):
            for c0 in range(0, n_cols, STAGE_COLS):
                slot = n_ring % STAGE_SLOTS
                n_ring += 1
                window = (pl.ds(r0, STAGE_ROWS), pl.ds(c0, STAGE_COLS))
                copy = pltpu.make_async_copy(w_hbm[k].at[window], stage.at[slot],
                                             stage_sems.at[slot])
                jobs.append((copy, w_bf16[k].at[window], stage.at[slot]))
    return jobs


def _convert_weights(jobs):
    ahead = STAGE_SLOTS - 1
    for copy, _, _ in jobs[:ahead]:
        copy.start()
    for n, (copy, dst, staged) in enumerate(jobs):
        if n + ahead < len(jobs):
            jobs[n + ahead][0].start()
        copy.wait()
        dst[...] = staged[...].astype(jnp.bfloat16)


def _sample_kernel(x_ref, zs_ref, us_ref, g_pre_mix, b_gate, w_conv, pool_scale, g_post_mix,
                   g_pre_mlp, g_post_mlp, *refs):
    n_w = len(MATMUL_WEIGHTS)
    w_hbm = refs[:n_w]
    y_ref, conv_ref, pool_ref = refs[n_w:n_w + 3]
    w_out_hbm = refs[n_w + 3:2 * n_w + 3]
    w_bf16 = refs[2 * n_w + 3:3 * n_w + 3]
    stage, narrow_stage, stage_sems, narrow_sem, out_sems = refs[3 * n_w + 3:]
    i = pl.program_id(0)
    w_in, w_out_conv, w_pool, w_o, w_up, w_down = w_bf16

    def out_copy(k):
        return pltpu.make_async_copy(w_bf16[k], w_out_hbm[k], out_sems.at[k])

    @pl.when(i == 0)
    def _():
        _convert_weights(
            _weight_copy_jobs(w_hbm, w_bf16, stage, stage_sems, narrow_stage, narrow_sem))
        for k in range(n_w):
            out_copy(k).start(priority=1)

    nseq, t_len, _ = x_ref.shape
    x = jnp.concatenate([x_ref[:, t, :] for t in range(t_len)], axis=0)
    z_hist = jnp.concatenate([zs_ref[:, k, :] for k in range(CONV_WIDTH - 1)], axis=0)
    u_hist = us_ref[...].reshape(POOL_BUF * nseq, D_MODEL)

    def emit_z(z):
        for k in range(CONV_WIDTH - 1):
            t = t_len - (CONV_WIDTH - 1) + k
            conv_ref[:, k, :] = z[t * nseq:(t + 1) * nseq]

    def emit_uext(uext):
        pool_ref[...] = uext[t_len * nseq:].reshape(POOL_BUF, nseq, D_MODEL)

    st = dict(x=x, z_hist=z_hist, u_hist=u_hist, time_major=True, start_pos=PAST_LEN, nseq=nseq,
              emit_z=emit_z, emit_uext=emit_uext)
    w = (g_pre_mix, w_in, b_gate, w_conv, w_out_conv, w_pool, pool_scale, w_o,
         g_post_mix, g_pre_mlp, w_up, w_down, g_post_mlp)
    _run_layer([st], w)
    y = st["y"]
    for t in range(t_len):
        y_ref[:, t, :] = y[t * nseq:(t + 1) * nseq]

    @pl.when(i == pl.num_programs(0) - 1)
    def _():
        for k in range(n_w):
            out_copy(k).wait()


def _const_spec(shape):
    zeros = (0,) * len(shape)
    return pl.BlockSpec(shape, lambda i: zeros, pipeline_mode=pl.Buffered(1))


def _prompt_call(x, weights):
    batch, seq, d = x.shape
    tm = PROMPT_TILE
    nb = PROMPT_SEQS
    f32 = jnp.float32
    vmem = pl.BlockSpec(memory_space=pltpu.VMEM)
    hbm = pl.BlockSpec(memory_space=pl.ANY)
    return pl.pallas_call(
        _prompt_kernel,
        in_specs=[hbm] + [vmem] * len(weights),
        out_specs=[hbm, vmem, vmem],
        out_shape=[
            jax.ShapeDtypeStruct((batch, seq, d), f32),
            jax.ShapeDtypeStruct((1, batch, CONV_WIDTH - 1, d), f32),
            jax.ShapeDtypeStruct((POOL_BUF, batch, d), f32),
        ],
        scratch_shapes=[pltpu.VMEM((2, nb, tm, d), f32),
                        pltpu.VMEM((2, nb, tm, d), f32),
                        pltpu.VMEM((nb, CONV_HIST, d), f32),
                        pltpu.VMEM((nb, POOL_HIST, d), f32),
                        pltpu.SemaphoreType.DMA((2,)),
                        pltpu.SemaphoreType.DMA((2,))],
        compiler_params=pltpu.CompilerParams(vmem_limit_bytes=VMEM_LIMIT_BYTES),
        name="prompt_layer",
    )(x, *weights)


def _sample_call(x, state_conv, state_pool_tm, vectors, w_f32):
    nseq_total, t_len, d = x.shape
    assert CONV_WIDTH - 1 <= t_len <= POOL_BUF
    ns = SAMPLE_SEQS
    f32, bf16 = jnp.float32, jnp.bfloat16
    any_spec = pl.BlockSpec(memory_space=pl.ANY)
    w_shapes = [shape for _, shape in MATMUL_WEIGHTS]
    narrow_shapes = [shape for shape in w_shapes if shape[1] % STAGE_COLS]
    assert len(narrow_shapes) == 1
    return pl.pallas_call(
        _sample_kernel,
        grid=(nseq_total // ns,),
        in_specs=([
            pl.BlockSpec((ns, t_len, d), lambda i: (i, 0, 0)),
            pl.BlockSpec((None, ns, CONV_WIDTH - 1, d), lambda i: (0, i, 0, 0)),
            pl.BlockSpec((POOL_BUF, ns, d), lambda i: (0, i, 0)),
        ] + [_const_spec(v.shape) for v in vectors] + [any_spec] * len(w_f32)),
        out_specs=[
            pl.BlockSpec((ns, t_len, d), lambda i: (i, 0, 0)),
            pl.BlockSpec((None, ns, CONV_WIDTH - 1, d), lambda i: (0, i, 0, 0)),
            pl.BlockSpec((POOL_BUF, ns, d), lambda i: (0, i, 0)),
        ] + [any_spec] * len(w_f32),
        out_shape=[
            jax.ShapeDtypeStruct((nseq_total, t_len, d), f32),
            jax.ShapeDtypeStruct((1, nseq_total, CONV_WIDTH - 1, d), f32),
            jax.ShapeDtypeStruct((POOL_BUF, nseq_total, d), f32),
        ] + [jax.ShapeDtypeStruct(s, bf16) for s in w_shapes],
        scratch_shapes=([pltpu.VMEM(s, bf16) for s in w_shapes]
                        + [pltpu.VMEM((STAGE_SLOTS, STAGE_ROWS, STAGE_COLS), f32),
                           pltpu.VMEM(narrow_shapes[0], f32),
                           pltpu.SemaphoreType.DMA((STAGE_SLOTS,)),
                           pltpu.SemaphoreType.DMA((1,)),
                           pltpu.SemaphoreType.DMA((len(w_f32),))]),
        compiler_params=pltpu.CompilerParams(
            dimension_semantics=("arbitrary",),
            vmem_limit_bytes=VMEM_LIMIT_BYTES),
        name="sample_layer",
    )(x, state_conv, state_pool_tm, *vectors, *w_f32)


def kernel(x_prompt, x_sample, state_conv, state_pool, g_pre_mix, w_in, b_gate, w_conv,
           w_out_conv, w_pool_group, pool_scale, w_o, g_post_mix, g_pre_mlp, w_up, w_down,
           g_post_mlp):
    depth = w_in.shape[0]
    assert depth == 1, "single-layer step"
    d = D_MODEL
    w_conv_flat = w_conv.reshape(1, CONV_WIDTH * d)
    vectors = (g_pre_mix, b_gate, w_conv_flat, pool_scale, g_post_mix, g_pre_mlp, g_post_mlp)
    w_f32 = (w_in[0], w_out_conv[0], w_pool_group[0].reshape(MATMUL_WEIGHTS[2][1]), w_o[0],
             w_up[0], w_down[0])

    state_pool_tm = jnp.transpose(state_pool[0], (1, 0, 2))
    y_sample, new_conv_sample, pool_tm, *w_bf16 = _sample_call(
        x_sample, state_conv, state_pool_tm, vectors, w_f32)
    new_pool_sample = jnp.transpose(pool_tm, (1, 0, 2))[None]

    wb_in, wb_out_conv, wb_pool, wb_o, wb_up, wb_down = w_bf16
    weights = (g_pre_mix, wb_in, b_gate, w_conv_flat, wb_out_conv, wb_pool, pool_scale, wb_o,
               g_post_mix, g_pre_mlp, wb_up, wb_down, g_post_mlp)
    y_prompt, new_conv_prompt, pool_tm = _prompt_call(x_prompt, weights)
    new_pool_prompt = jnp.transpose(pool_tm, (1, 0, 2))[None]
    return (y_prompt, y_sample, new_conv_prompt, new_pool_prompt, new_conv_sample,
            new_pool_sample)
```
